```python
import jax, jax.numpy as jnp
from jax import lax
import numpy as np

D_MODEL = 1024
BATCH = 2
SEQ = 16384
DEPTH = 4

N_TOKEN_MIXERS = 2
N_MLSTM_LAYERS = (DEPTH + N_TOKEN_MIXERS - 1) // N_TOKEN_MIXERS
N_POOL_LAYERS = DEPTH // N_TOKEN_MIXERS
EXPAND = 2
E_A = EXPAND * D_MODEL
N_HEADS_A = 8
DV_A = E_A // N_HEADS_A
DK_A = DV_A // 2
CONV_K = 4
CHUNK = 128
A_IN_COLS = 3 * E_A + 2 * N_HEADS_A
E_B = EXPAND * D_MODEL
POOL_WINDOWS = (2, 4, 8, 16)
N_GROUPS_B = len(POOL_WINDOWS)
CG_B = E_B // N_GROUPS_B
B_IN_COLS = 2 * E_B
LN_EPS = 1e-5
DEEPNORM_ALPHA = (2.0 * DEPTH) ** 0.25
DEEPNORM_BETA = (8.0 * DEPTH) ** -0.25

kernel_name = "hybrid_mlstm_multiscale_pool_deepnorm"


def _layer_norm(x, g, b):
    xf = x.astype(jnp.float32)
    mu = xf.mean(-1, keepdims=True)
    var = jnp.square(xf - mu).mean(-1, keepdims=True)
    y = (xf - mu) * lax.rsqrt(var + LN_EPS) * g.astype(jnp.float32) + b.astype(jnp.float32)
    return y.astype(x.dtype)


def _causal_dwconv(u, w, b):
    y = lax.conv_general_dilated(
        u, w[:, None, :], window_strides=(1,), padding=[(CONV_K - 1, 0)],
        dimension_numbers=("NWC", "WIO", "NWC"), feature_group_count=u.shape[-1])
    return y + b


def _mlstm_chunkwise(q, k, v, log_i, log_f):
    B, S, H, DK = q.shape
    DV = v.shape[-1]
    NC = S // CHUNK

    def to_chunks(a):
        a = a.reshape((B, NC, CHUNK, H) + a.shape[3:])
        return jnp.moveaxis(jnp.moveaxis(a, 1, 0), 3, 2)

    qc, kc, vc = to_chunks(q), to_chunks(k), to_chunks(v)
    lic, lfc = to_chunks(log_i), to_chunks(log_f)
    causal = jnp.tril(jnp.ones((CHUNK, CHUNK), dtype=bool))

    def step(carry, inp):
        C, n, m = carry
        qq, kk, vv, li, lf = inp
        bcum = jnp.cumsum(lf, axis=-1)
        g = bcum + m[..., None]
        D = bcum[..., :, None] - bcum[..., None, :] + li[..., None, :]
        D = jnp.where(causal, D, -jnp.inf)
        m_t = jnp.maximum(g, D.max(-1))
        w_intra = jnp.exp(D - m_t[..., None])
        w_inter = jnp.exp(g - m_t)
        s = jnp.einsum("bhtd,bhsd->bhts", qq, kk) * w_intra
        num = (jnp.einsum("bhts,bhsv->bhtv", s, vv)
               + w_inter[..., None] * jnp.einsum("bhtd,bhdv->bhtv", qq, C))
        den = s.sum(-1) + w_inter * jnp.einsum("bhtd,bhd->bht", qq, n)
        h = num / jnp.maximum(jnp.abs(den), jnp.exp(-m_t))[..., None]
        bL = bcum[..., -1]
        a = bL[..., None] - bcum + li
        m_new = jnp.maximum(bL + m, a.max(-1))
        w_c = jnp.exp(bL + m - m_new)
        w_s = jnp.exp(a - m_new[..., None])
        C_new = w_c[..., None, None] * C + jnp.einsum("bhs,bhsd,bhsv->bhdv", w_s, kk, vv)
        n_new = w_c[..., None] * n + jnp.einsum("bhs,bhsd->bhd", w_s, kk)
        return (C_new, n_new, m_new), h

    init = (jnp.zeros((B, H, DK, DV), jnp.float32),
            jnp.zeros((B, H, DK), jnp.float32),
            jnp.zeros((B, H), jnp.float32))
    _, hc = lax.scan(step, init, (qc, kc, vc, lic, lfc))
    return jnp.moveaxis(hc, 2, 3).transpose(1, 0, 2, 3, 4).reshape(B, S, H, DV)


def _mlstm_branch(x, w_in, conv_w, conv_b, w_q, w_k, w_v, b_i, b_f, norm_g, skip, w_out):
    B, S, _ = x.shape
    H = N_HEADS_A
    proj = x @ w_in
    xm = proj[..., :E_A]
    z = proj[..., E_A:2 * E_A]
    o_pre = proj[..., 2 * E_A:3 * E_A]
    i_pre = (proj[..., 3 * E_A:3 * E_A + H] + b_i).astype(jnp.float32)
    f_pre = (proj[..., 3 * E_A + H:] + b_f).astype(jnp.float32)
    xc = jax.nn.silu(_causal_dwconv(xm, conv_w, conv_b))
    xch = xc.reshape(B, S, H, DV_A)
    xmh = xm.reshape(B, S, H, DV_A)
    q = jnp.einsum("bshc,hcd->bshd", xch, w_q).astype(jnp.float32) * (DK_A ** -0.5)
    k = jnp.einsum("bshc,hcd->bshd", xch, w_k).astype(jnp.float32)
    v = jnp.einsum("bshc,hcv->bshv", xmh, w_v).astype(jnp.float32)
    h = _mlstm_chunkwise(q, k, v, i_pre, jax.nn.log_sigmoid(f_pre))
    mu = h.mean(-1, keepdims=True)
    var = jnp.square(h - mu).mean(-1, keepdims=True)
    hn = ((h - mu) * lax.rsqrt(var + LN_EPS)).reshape(B, S, E_A) * norm_g.astype(jnp.float32)
    hb = (jax.nn.sigmoid(o_pre.astype(jnp.float32)) * hn).astype(x.dtype) + skip * xc
    return (hb * jax.nn.silu(z)) @ w_out


def _pool_branch(x, w_in, w_group, scale, w_out):
    B, S, _ = x.shape
    proj = x @ w_in
    u = proj[..., :E_B]
    z = proj[..., E_B:]
    uf = u.astype(jnp.float32)
    csum = jnp.cumsum(uf, axis=1)
    pos = jnp.arange(S)
    means = []
    for gi, win in enumerate(POOL_WINDOWS):
        cg = csum[..., gi * CG_B:(gi + 1) * CG_B]
        prev = jnp.pad(cg, ((0, 0), (win, 0), (0, 0)))[:, :S]
        cnt = jnp.minimum(pos + 1, win).astype(jnp.float32)[None, :, None]
        means.append((cg - prev) / cnt)
    pooled = (jnp.concatenate(means, axis=-1) - uf).astype(x.dtype)
    pooled = pooled.reshape(B, S, N_GROUPS_B, CG_B)
    mixed = jnp.einsum("bsgc,gcd->bsgd", pooled, w_group).reshape(B, S, E_B) * scale
    return (mixed * jax.nn.silu(z)) @ w_out


def setup_inputs(seed: int = 0) -> dict:
    key = jax.random.key(seed)
    ks = jax.random.split(key, 20)
    NA, NB, H = N_MLSTM_LAYERS, N_POOL_LAYERS, N_HEADS_A
    nrm = lambda k, shape, s: jax.random.normal(k, shape, jnp.float32) * s
    f_bias = jnp.broadcast_to(jnp.linspace(3.0, 6.0, H, dtype=jnp.float32), (NA, H))
    return {
        "x": nrm(ks[0], (BATCH, SEQ, D_MODEL), 1.0),
        "mlstm_w_in": nrm(ks[1], (NA, D_MODEL, A_IN_COLS), D_MODEL ** -0.5),
        "mlstm_conv_w": nrm(ks[2], (NA, CONV_K, E_A), CONV_K ** -0.5),
        "mlstm_conv_b": nrm(ks[3], (NA, E_A), 0.01),
        "mlstm_w_q": nrm(ks[4], (NA, H, DV_A, DK_A), DV_A ** -0.5),
        "mlstm_w_k": nrm(ks[5], (NA, H, DV_A, DK_A), DV_A ** -0.5),
        "mlstm_w_v": nrm(ks[6], (NA, H, DV_A, DV_A), DV_A ** -0.5),
        "mlstm_b_i": nrm(ks[7], (NA, H), 0.1),
        "mlstm_b_f": f_bias + nrm(ks[8], (NA, H), 0.1),
        "mlstm_norm_g": 1.0 + nrm(ks[9], (NA, E_A), 0.02),
        "mlstm_skip": 1.0 + nrm(ks[10], (NA, E_A), 0.02),
        "mlstm_w_out": nrm(ks[11], (NA, E_A, D_MODEL), (E_A ** -0.5) * DEEPNORM_BETA),
        "pool_w_in": nrm(ks[12], (NB, D_MODEL, B_IN_COLS), D_MODEL ** -0.5),
        "pool_w_group": nrm(ks[13], (NB, N_GROUPS_B, CG_B, CG_B), CG_B ** -0.5),
        "pool_scale": 1.0 + nrm(ks[14], (NB, E_B), 0.02),
        "pool_w_out": nrm(ks[15], (NB, E_B, D_MODEL), (E_B ** -0.5) * DEEPNORM_BETA),
        "post_ln_g": 1.0 + nrm(ks[16], (DEPTH, D_MODEL), 0.02),
        "post_ln_b": nrm(ks[17], (DEPTH, D_MODEL), 0.02),
    }


def reference(x, mlstm_w_in, mlstm_conv_w, mlstm_conv_b, mlstm_w_q, mlstm_w_k, mlstm_w_v,
              mlstm_b_i, mlstm_b_f, mlstm_norm_g, mlstm_skip, mlstm_w_out,
              pool_w_in, pool_w_group, pool_scale, pool_w_out, post_ln_g, post_ln_b):
    for i in range(DEPTH):
        j = i // N_TOKEN_MIXERS
        if i % N_TOKEN_MIXERS == 0:
            branch = _mlstm_branch(x, mlstm_w_in[j], mlstm_conv_w[j], mlstm_conv_b[j],
                                   mlstm_w_q[j], mlstm_w_k[j], mlstm_w_v[j],
                                   mlstm_b_i[j], mlstm_b_f[j], mlstm_norm_g[j],
                                   mlstm_skip[j], mlstm_w_out[j])
        else:
            branch = _pool_branch(x, pool_w_in[j], pool_w_group[j], pool_scale[j], pool_w_out[j])
        x = _layer_norm(DEEPNORM_ALPHA * x + branch, post_ln_g[i], post_ln_b[i])
    return x
```

```python
import functools

import jax
import jax.numpy as jnp
from jax import lax
from jax.experimental import pallas as pl
from jax.experimental.pallas import tpu as pltpu

N_HEADS = 8
CONV_K = 4
CHUNK = 128
POOL_WINDOWS = (2, 4, 8, 16)
LN_EPS = 1e-5
TILE = 512
CONV_HIST = 8
POOL_HIST = 16
MXU_DTYPE = jnp.bfloat16
VMEM_LIMIT_BYTES = 52 * 1024 * 1024


def _sigmoid(x):
    return 1.0 / (1.0 + jnp.exp(-x))


def _silu(x):
    return x * _sigmoid(x)


def _log_sigmoid(x):
    return jnp.minimum(x, 0.0) - jnp.log1p(jnp.exp(-jnp.abs(x)))


def _dot(a, b):
    return jnp.dot(a.astype(MXU_DTYPE), b.astype(MXU_DTYPE), preferred_element_type=jnp.float32)


def _dot_nt(a, b):
    return lax.dot_general(a.astype(MXU_DTYPE), b.astype(MXU_DTYPE),
                           (((1,), (1,)), ((), ())), preferred_element_type=jnp.float32)


def _lane_prefix(x, combine, identity):
    lane = lax.broadcasted_iota(jnp.int32, x.shape, 1)
    shift = 1
    while shift < x.shape[1]:
        shifted = pltpu.roll(x, shift, 1)
        x = combine(x, jnp.where(lane >= shift, shifted, identity))
        shift *= 2
    return x


def _last_lane(x):
    return jnp.broadcast_to(x[:, x.shape[1] - 1:], x.shape)


def _column_of(row):
    n = row.shape[1]
    return jnp.broadcast_to(row, (n, n)).T


def _post_norm(x, branch, g, b, alpha):
    y = alpha * x + branch
    mu = jnp.mean(y, axis=-1, keepdims=True)
    cen = y - mu
    var = jnp.mean(cen * cen, axis=-1, keepdims=True)
    return cen * lax.rsqrt(var + LN_EPS) * g + b


def _mlstm_layer_kernel(x_ref, w_in_ref, wg_ref, bg_ref, cw_ref, cb_ref, wq_ref, wkt_ref, wv_ref,
                        ng_ref, sk_ref, wo_ref, lg_ref, lb_ref, o_ref,
                        xb_s, pr_s, xc_s, q_s, kt_s, v_s, g_s, acc_s,
                        c_s, mx_s, en_s, ws_s, wc_s, mp_s,
                        m_st, cst_s, nst_s, hist_s, *, alpha, q_scale):
    n_chunks = TILE // CHUNK
    dv = v_s.shape[1]
    first_tile = pl.program_id(1) == 0

    @pl.when(first_tile)
    def _():
        m_st[...] = jnp.zeros_like(m_st)
        cst_s[...] = jnp.zeros_like(cst_s)
        nst_s[...] = jnp.zeros_like(nst_s)
        hist_s[...] = jnp.zeros_like(hist_s)

    xb_s[...] = x_ref[...].astype(xb_s.dtype)

    gates = _dot_nt(wg_ref[...], xb_s[...]) + bg_ref[:, 0:1]
    log_i = gates[:N_HEADS]
    log_f = _log_sigmoid(gates[N_HEADS:])
    m_prev = m_st[...]
    for c in range(n_chunks):
        lanes = slice(c * CHUNK, (c + 1) * CHUNK)
        bcum = _lane_prefix(log_f[:, lanes], jnp.add, 0.0)
        cc = log_i[:, lanes] - bcum
        mx = jnp.maximum(m_prev, _lane_prefix(cc, jnp.maximum, -jnp.inf))
        mx_last = _last_lane(mx)
        rows = (cc, mx, jnp.exp(-(bcum + mx)), jnp.exp(cc - mx_last),
                jnp.exp(m_prev - mx_last), m_prev)
        for dst, val in zip((c_s, mx_s, en_s, ws_s, wc_s, mp_s), rows):
            for hh in range(N_HEADS):
                dst[c, hh] = val[hh:hh + 1]
        m_prev = _last_lane(bcum) + mx_last
    m_st[...] = m_prev

    acc_s[...] = jnp.zeros_like(acc_s)
    tri = (lax.broadcasted_iota(jnp.int32, (CHUNK, CHUNK), 0)
           >= lax.broadcasted_iota(jnp.int32, (CHUNK, CHUNK), 1))

    def head_body(h, carry):
        pr_s[0:CONV_HIST, 0:dv] = hist_s[h]
        pr_s[CONV_HIST:, :] = _dot(xb_s[...], w_in_ref[h])
        hist_s[h] = pr_s[TILE:TILE + CONV_HIST, 0:dv]
        cw = cw_ref[h]
        conv = cb_ref[h]
        for k in range(CONV_K):
            start = CONV_HIST - (CONV_K - 1) + k
            conv = conv + cw[k:k + 1] * pr_s[start:start + TILE, 0:dv]
        xc = _silu(conv)
        xc_s[...] = xc
        xc_b = xc.astype(MXU_DTYPE)
        q_s[...] = _dot(xc_b, wq_ref[h]) * q_scale
        kt_s[...] = _dot_nt(wkt_ref[h], xc_b)
        v_s[...] = _dot(pr_s[CONV_HIST:, 0:dv], wv_ref[h]).astype(v_s.dtype)

        for c in range(n_chunks):
            rows = slice(c * CHUNK, (c + 1) * CHUNK)
            mx_col = _column_of(mx_s[c, h])
            en_col = _column_of(en_s[c, h])
            decay = jnp.where(tri, jnp.exp(c_s[c, h] - mx_col), 0.0)
            qc = q_s[rows, :]
            ktc = kt_s[:, rows]
            vc = v_s[rows, :]
            s = _dot(qc, ktc) * decay
            qs = qc * jnp.exp(mp_s[c, h] - mx_col)
            c_prev = cst_s[h]
            n_prev = nst_s[h]
            den = jnp.sum(s + qs * n_prev, axis=-1, keepdims=True)
            num = _dot(jnp.concatenate([s, qs], axis=1),
                       jnp.concatenate([vc, c_prev.astype(MXU_DTYPE)], axis=0))
            hv = num / jnp.maximum(jnp.abs(den), en_col[:, 0:1])
            kts = ktc * ws_s[c, h]
            wc = wc_s[c, h][:, 0:1]
            cst_s[h] = wc * c_prev + _dot(kts, vc)
            n_col = jnp.sum(kts, axis=-1, keepdims=True)
            nst_s[h] = wc * n_prev + _column_of_transposed(n_col)
            mu = jnp.mean(hv, axis=-1, keepdims=True)
            cen = hv - mu
            var = jnp.mean(cen * cen, axis=-1, keepdims=True)
            hn = cen * lax.rsqrt(var + LN_EPS) * ng_ref[h]
            og = _sigmoid(pr_s[CONV_HIST + c * CHUNK:CONV_HIST + (c + 1) * CHUNK, 2 * dv:3 * dv])
            hb = og * hn + sk_ref[h] * xc_s[rows, :]
            z = pr_s[CONV_HIST + c * CHUNK:CONV_HIST + (c + 1) * CHUNK, dv:2 * dv]
            g_s[rows, :] = (hb * _silu(z)).astype(g_s.dtype)
        acc_s[...] += _dot(g_s[...], wo_ref[h])
        return carry

    lax.fori_loop(0, N_HEADS, head_body, 0)
    o_ref[...] = _post_norm(x_ref[...], acc_s[...], lg_ref[...], lb_ref[...], alpha)


def _column_of_transposed(col):
    n = col.shape[0]
    return jnp.broadcast_to(col, (n, n)).T[0:1]


def _resident(shape):
    zeros = (0,) * len(shape)
    return pl.BlockSpec(shape, lambda b, i: zeros, pipeline_mode=pl.Buffered(1))


def _mlstm_layer(x, w_in, conv_w, conv_b, w_q, w_k, w_v, b_i, b_f, norm_g, skip, w_out,
                 ln_g, ln_b, alpha):
    B, S, D = x.shape
    H = N_HEADS
    dv = w_v.shape[-1]
    dk = w_q.shape[-1]
    E = H * dv
    assert S % TILE == 0 and TILE % CHUNK == 0 and CHUNK == dk
    wdt = MXU_DTYPE
    w_heads = (w_in[:, :3 * E].reshape(D, 3, H, dv).transpose(2, 0, 1, 3)
               .reshape(H, D, 3 * dv).astype(wdt))
    w_gate = w_in[:, 3 * E:].T.astype(wdt)
    b_gate = jnp.broadcast_to(jnp.concatenate([b_i, b_f])[:, None], (2 * H, 128)).astype(jnp.float32)
    args = (
        x, w_heads, w_gate, b_gate,
        conv_w.reshape(CONV_K, H, dv).transpose(1, 0, 2), conv_b.reshape(H, 1, dv),
        w_q.astype(wdt), w_k.transpose(0, 2, 1).astype(wdt), w_v.astype(wdt),
        norm_g.reshape(H, 1, dv), skip.reshape(H, 1, dv),
        w_out.reshape(H, dv, D).astype(wdt), ln_g.reshape(1, D), ln_b.reshape(1, D),
    )
    x_spec = pl.BlockSpec((None, TILE, D), lambda b, i: (b, i, 0))
    in_specs = [x_spec] + [_resident(a.shape) for a in args[1:]]
    n_chunks = TILE // CHUNK
    row_scratch = pltpu.VMEM((n_chunks, H, 1, CHUNK), jnp.float32)
    scratch = [
        pltpu.VMEM((TILE, D), wdt),
        pltpu.VMEM((TILE + CONV_HIST, 3 * dv), jnp.float32),
        pltpu.VMEM((TILE, dv), jnp.float32),
        pltpu.VMEM((TILE, dk), jnp.float32),
        pltpu.VMEM((dk, TILE), jnp.float32),
        pltpu.VMEM((TILE, dv), wdt),
        pltpu.VMEM((TILE, dv), wdt),
        pltpu.VMEM((TILE, D), jnp.float32),
        row_scratch, row_scratch, row_scratch, row_scratch, row_scratch, row_scratch,
        pltpu.VMEM((H, CHUNK), jnp.float32),
        pltpu.VMEM((H, dk, dv), jnp.float32),
        pltpu.VMEM((H, 1, dk), jnp.float32),
        pltpu.VMEM((H, CONV_HIST, dv), jnp.float32),
    ]
    kern = functools.partial(_mlstm_layer_kernel, alpha=alpha, q_scale=float(dk) ** -0.5)
    return pl.pallas_call(
        kern,
        grid=(B, S // TILE),
        in_specs=in_specs,
        out_specs=pl.BlockSpec((None, TILE, D), lambda b, i: (b, i, 0)),
        out_shape=jax.ShapeDtypeStruct((B, S, D), x.dtype),
        scratch_shapes=scratch,
        compiler_params=pltpu.CompilerParams(
            dimension_semantics=("arbitrary", "arbitrary"), vmem_limit_bytes=VMEM_LIMIT_BYTES),
        name="mlstm_layer",
    )(*args)


def _pool_layer_kernel(x_ref, w_in_ref, wg_ref, sc_ref, wo_ref, lg_ref, lb_ref, o_ref,
                       xb_s, u_s, pooled_s, acc_s, hist_s, *, alpha):
    cg = u_s.shape[1]
    first_tile = pl.program_id(1) == 0

    @pl.when(first_tile)
    def _():
        hist_s[...] = jnp.zeros_like(hist_s)

    xb_s[...] = x_ref[...].astype(xb_s.dtype)
    acc_s[...] = jnp.zeros_like(acc_s)

    for g, win in enumerate(POOL_WINDOWS):
        proj = _dot(xb_s[...], w_in_ref[g])
        u = proj[:, :cg]
        z = proj[:, cg:]
        u_s[0:POOL_HIST, :] = hist_s[g]
        u_s[POOL_HIST:, :] = u
        hist_s[g] = u_s[TILE:TILE + POOL_HIST, :]
        wsum = u_s[...]
        span = 1
        while span < win:
            wsum = wsum + pltpu.roll(wsum, span, 0)
            span *= 2
        wsum = wsum[POOL_HIST:]
        pooled_s[...] = wsum * (1.0 / win) - u

        @pl.when(first_tile)
        def _():
            cnt = jnp.minimum(lax.broadcasted_iota(jnp.int32, (POOL_HIST, cg), 0) + 1, win)
            pooled_s[0:POOL_HIST, :] = wsum[:POOL_HIST] / cnt.astype(jnp.float32) - u[:POOL_HIST]

        mixed = _dot(pooled_s[...], wg_ref[g]) * sc_ref[g]
        acc_s[...] += _dot(mixed * _silu(z), wo_ref[g])

    o_ref[...] = _post_norm(x_ref[...], acc_s[...], lg_ref[...], lb_ref[...], alpha)


def _pool_layer(x, w_in, w_group, scale, w_out, ln_g, ln_b, alpha):
    B, S, D = x.shape
    G, cg, _ = w_group.shape
    E = G * cg
    assert S % TILE == 0 and G == len(POOL_WINDOWS) and max(POOL_WINDOWS) <= POOL_HIST
    wdt = MXU_DTYPE
    w_groups = (w_in.reshape(D, 2, G, cg).transpose(2, 0, 1, 3).reshape(G, D, 2 * cg).astype(wdt))
    args = (x, w_groups, w_group.astype(wdt), scale.reshape(G, 1, cg),
            w_out.reshape(G, cg, D).astype(wdt), ln_g.reshape(1, D), ln_b.reshape(1, D))
    in_specs = ([pl.BlockSpec((None, TILE, D), lambda b, i: (b, i, 0))]
                + [_resident(a.shape) for a in args[1:]])
    scratch = [
        pltpu.VMEM((TILE, D), wdt),
        pltpu.VMEM((TILE + POOL_HIST, cg), jnp.float32),
        pltpu.VMEM((TILE, cg), jnp.float32),
        pltpu.VMEM((TILE, D), jnp.float32),
        pltpu.VMEM((G, POOL_HIST, cg), jnp.float32),
    ]
    return pl.pallas_call(
        functools.partial(_pool_layer_kernel, alpha=alpha),
        grid=(B, S // TILE),
        in_specs=in_specs,
        out_specs=pl.BlockSpec((None, TILE, D), lambda b, i: (b, i, 0)),
        out_shape=jax.ShapeDtypeStruct((B, S, D), x.dtype),
        scratch_shapes=scratch,
        compiler_params=pltpu.CompilerParams(
            dimension_semantics=("arbitrary", "arbitrary"), vmem_limit_bytes=VMEM_LIMIT_BYTES),
        name="pool_layer",
    )(*args)


def kernel(x, mlstm_w_in, mlstm_conv_w, mlstm_conv_b, mlstm_w_q, mlstm_w_k, mlstm_w_v, mlstm_b_i,
           mlstm_b_f, mlstm_norm_g, mlstm_skip, mlstm_w_out, pool_w_in, pool_w_group, pool_scale,
           pool_w_out, post_ln_g, post_ln_b):
    depth = post_ln_g.shape[0]
    alpha = (2.0 * depth) ** 0.25
    for i in range(depth):
        j = i // 2
        if i % 2 == 0:
            x = _mlstm_layer(x, mlstm_w_in[j], mlstm_conv_w[j], mlstm_conv_b[j], mlstm_w_q[j],
                             mlstm_w_k[j], mlstm_w_v[j], mlstm_b_i[j], mlstm_b_f[j],
                             mlstm_norm_g[j], mlstm_skip[j], mlstm_w_out[j],
                             post_ln_g[i], post_ln_b[i], alpha)
        else:
            x = _pool_layer(x, pool_w_in[j], pool_w_group[j], pool_scale[j], pool_w_out[j],
                            post_ln_g[i], post_ln_b[i], alpha)
    return x
```

```python
import functools

import jax
import jax.numpy as jnp
from jax import lax
from jax.experimental import pallas as pl
from jax.experimental.pallas import tpu as pltpu

N_HEADS = 8
CONV_K = 4
CHUNK = 128
POOL_WINDOWS = (2, 4, 8, 16)
LN_EPS = 1e-5
TILE = 512
CONV_HIST = 8
POOL_HIST = 16
MXU_DTYPE = jnp.bfloat16
VMEM_LIMIT_BYTES = 52 * 1024 * 1024


NEG_LOG2_E = -1.4426950408889634


def _sigmoid(x):
    return 1.0 / (1.0 + jnp.exp2(x * NEG_LOG2_E))


def _silu(x):
    return x * _sigmoid(x)


def _log_sigmoid(x):
    return jnp.minimum(x, 0.0) - jnp.log1p(jnp.exp(-jnp.abs(x)))


def _dot(a, b):
    return jnp.dot(a.astype(MXU_DTYPE), b.astype(MXU_DTYPE), preferred_element_type=jnp.float32)


def _dot_nt(a, b):
    return lax.dot_general(a.astype(MXU_DTYPE), b.astype(MXU_DTYPE),
                           (((1,), (1,)), ((), ())), preferred_element_type=jnp.float32)


def _dot_f32_lhs(a, b):
    r = a.shape[0]
    hi = a.astype(jnp.bfloat16).astype(jnp.float32)
    rest = a - hi
    mid = rest.astype(jnp.bfloat16).astype(jnp.float32)
    parts = jnp.concatenate([hi, mid, rest - mid], axis=0).astype(jnp.bfloat16)
    out = jnp.dot(parts, b.astype(jnp.bfloat16), preferred_element_type=jnp.float32)
    return out[:r] + out[r:2 * r] + out[2 * r:]


def _lane_prefix_max(x):
    lane = lax.broadcasted_iota(jnp.int32, x.shape, 1)
    shift = 1
    while shift < x.shape[1]:
        x = jnp.maximum(x, jnp.where(lane >= shift, pltpu.roll(x, shift, 1), -jnp.inf))
        shift *= 2
    return x


def _column_of(row):
    n = row.shape[1]
    return jnp.broadcast_to(row, (n, n)).T


def _last_lane(x):
    return jnp.broadcast_to(x[:, x.shape[1] - 1:], x.shape)


def _row_of(col):
    n = col.shape[0]
    return jnp.broadcast_to(col, (n, n)).T[0:1]


def _post_norm(x, branch, g, b, alpha):
    y = alpha * x + branch
    mu = jnp.mean(y, axis=-1, keepdims=True)
    cen = y - mu
    var = jnp.mean(cen * cen, axis=-1, keepdims=True)
    return cen * lax.rsqrt(var + LN_EPS) * g + b


def _mlstm_layer_kernel(x_ref, w_in_ref, wg_ref, bg_ref, cw_ref, cb_ref, wq_ref, wkt_ref, wv_ref,
                        ng_ref, sk_ref, wo_ref, lg_ref, lb_ref, o_ref,
                        xb_s, pr_s, xc_s, q_s, kt_s, v_s, g_s,
                        c_s, mx_s, en_s, ws_s, wc_s, mp_s,
                        m_st, cst_s, nst_s, hist_s, *, alpha, q_scale):
    n_chunks = TILE // CHUNK
    dv = v_s.shape[1]
    first_tile = pl.program_id(1) == 0

    @pl.when(first_tile)
    def _():
        m_st[...] = jnp.zeros_like(m_st)
        cst_s[...] = jnp.zeros_like(cst_s)
        nst_s[...] = jnp.zeros_like(nst_s)
        hist_s[...] = jnp.zeros_like(hist_s)

    xb_s[...] = x_ref[...].astype(xb_s.dtype)

    gates = _dot_nt(wg_ref[...], xb_s[...]) + bg_ref[:, 0:1]
    log_i = gates[:N_HEADS]
    log_f = _log_sigmoid(gates[N_HEADS:])
    row_id = lax.broadcasted_iota(jnp.int32, (CHUNK, CHUNK), 0)
    col_id = lax.broadcasted_iota(jnp.int32, (CHUNK, CHUNK), 1)
    tri = row_id >= col_id
    lf_rows = jnp.concatenate([log_f[:, c * CHUNK:(c + 1) * CHUNK] for c in range(n_chunks)], axis=0)
    bcum_rows = _dot_f32_lhs(lf_rows, jnp.where(row_id <= col_id, 1.0, 0.0))
    li_rows = jnp.concatenate([log_i[:, c * CHUNK:(c + 1) * CHUNK] for c in range(n_chunks)], axis=0)
    cc_rows = li_rows - bcum_rows
    cmax_rows = _lane_prefix_max(cc_rows)
    m_prev = m_st[...]
    for c in range(n_chunks):
        blk = slice(c * N_HEADS, (c + 1) * N_HEADS)
        bcum = bcum_rows[blk]
        cc = cc_rows[blk]
        mx = jnp.maximum(m_prev, cmax_rows[blk])
        mx_last = _last_lane(mx)
        rows = (cc, mx, jnp.exp(-(bcum + mx)), jnp.exp(cc - mx_last), jnp.exp(m_prev - mx_last), m_prev)
        for dst, val in zip((c_s, mx_s, en_s, ws_s, wc_s, mp_s), rows):
            for hh in range(N_HEADS):
                dst[c, hh] = val[hh:hh + 1]
        m_prev = _last_lane(bcum) + mx_last
    m_st[...] = m_prev

    def project(h, slot, part):
        pr = pr_s.at[slot]
        cols = pl.ds(pl.multiple_of(part * N_HEADS * dv + h * dv, dv), dv)
        if part == 0:
            pr[0:CONV_HIST, 0:dv] = hist_s[h]
        pr[CONV_HIST:, part * dv:(part + 1) * dv] = _dot(xb_s[...], w_in_ref[:, cols])
        if part == 0:
            hist_s[h] = pr[TILE:TILE + CONV_HIST, 0:dv]

    def mix(h, slot, fillers=()):
        pr = pr_s.at[slot]
        cw = cw_ref[h]
        xe = pr[:, 0:dv]
        xe1 = pltpu.roll(xe, 1, 0)
        pair = cw[1:2] * xe + cw[0:1] * xe1
        conv = cb_ref[h] + cw[3:4] * xe + cw[2:3] * xe1 + pltpu.roll(pair, 2, 0)
        xc = _silu(conv[CONV_HIST:])
        xc_s[...] = xc
        xc_b = xc.astype(MXU_DTYPE)
        q_s[...] = _dot(xc_b, wq_ref[h]) * q_scale
        kt_s[...] = _dot_nt(wkt_ref[h], xc_b)
        v_s[...] = _dot(pr[CONV_HIST:, 0:dv], wv_ref[h]).astype(v_s.dtype)

        for c in range(n_chunks):
            rows = slice(c * CHUNK, (c + 1) * CHUNK)
            prow = slice(CONV_HIST + c * CHUNK, CONV_HIST + (c + 1) * CHUNK)
            m_in = mp_s[c, h][:, 0:1]
            mx_col = _column_of(mx_s[c, h])
            decay = jnp.where(tri, jnp.exp(c_s[c, h] - mx_col), 0.0)
            qc = q_s[rows, :]
            ktc = kt_s[:, rows]
            vc = v_s[rows, :]
            scores = _dot(qc, ktc)
            kts = ktc * ws_s[c, h]
            kv = _dot(kts, vc)
            if c < len(fillers):
                fillers[c]()
            s = scores * decay
            qs = qc * jnp.exp(m_in - mx_col)
            c_prev = cst_s[h]
            n_prev = nst_s[h]
            den = jnp.sum(s + qs * n_prev, axis=-1, keepdims=True)
            num = _dot(jnp.concatenate([s, qs], axis=1),
                       jnp.concatenate([vc, c_prev.astype(MXU_DTYPE)], axis=0))
            hv = num / jnp.maximum(jnp.abs(den), _column_of(en_s[c, h])[:, 0:1])
            wc = wc_s[c, h][:, 0:1]
            cst_s[h] = wc * c_prev + kv
            nst_s[h] = wc * n_prev + _row_of(jnp.sum(kts, axis=-1, keepdims=True))
            mu = jnp.mean(hv, axis=-1, keepdims=True)
            cen = hv - mu
            var = jnp.mean(cen * cen, axis=-1, keepdims=True)
            hn = cen * lax.rsqrt(var + LN_EPS) * ng_ref[h]
            hb = _sigmoid(pr[prow, 2 * dv:3 * dv]) * hn + sk_ref[h] * xc_s[rows, :]
            g_s[h, rows, :] = (hb * _silu(pr[prow, dv:2 * dv])).astype(g_s.dtype)

    def mix_and_project_next(h, slot):
        project(h + 1, 1 - slot, 0)
        mix(h, slot, [functools.partial(project, h + 1, 1 - slot, 1),
                      functools.partial(project, h + 1, 1 - slot, 2)])

    for part in range(3):
        project(0, 0, part)

    def heads_body(i, carry):
        mix_and_project_next(2 * i, 0)
        mix_and_project_next(2 * i + 1, 1)
        return carry

    lax.fori_loop(0, N_HEADS // 2 - 1, heads_body, 0)
    mix_and_project_next(N_HEADS - 2, 0)
    mix(N_HEADS - 1, 1)
    branch = _dot(g_s[0], wo_ref[0])
    for hh in range(1, N_HEADS):
        branch = branch + _dot(g_s[hh], wo_ref[hh])
    o_ref[...] = _post_norm(x_ref[...], branch, lg_ref[...], lb_ref[...], alpha)


def _resident(shape):
    zeros = (0,) * len(shape)
    return pl.BlockSpec(shape, lambda b, i: zeros, pipeline_mode=pl.Buffered(1))


def _mlstm_layer(x, w_in, conv_w, conv_b, w_q, w_k, w_v, b_i, b_f, norm_g, skip, w_out,
                 ln_g, ln_b, alpha):
    B, S, D = x.shape
    H = N_HEADS
    dv = w_v.shape[-1]
    dk = w_q.shape[-1]
    E = H * dv
    assert S % TILE == 0 and TILE % CHUNK == 0 and CHUNK == dk and H % 2 == 0
    assert conv_w.shape[0] == CONV_K == 4 and CONV_HIST >= CONV_K - 1
    wdt = MXU_DTYPE
    w_main = w_in[:, :3 * E].astype(wdt)
    w_gate = w_in[:, 3 * E:].T.astype(wdt)
    b_gate = jnp.broadcast_to(jnp.concatenate([b_i, b_f])[:, None], (2 * H, 128)).astype(jnp.float32)
    args = (
        x, w_main, w_gate, b_gate,
        conv_w.reshape(CONV_K, H, dv).transpose(1, 0, 2), conv_b.reshape(H, 1, dv),
        w_q.astype(wdt), w_k.transpose(0, 2, 1).astype(wdt), w_v.astype(wdt),
        norm_g.reshape(H, 1, dv), skip.reshape(H, 1, dv),
        w_out.reshape(H, dv, D).astype(wdt), ln_g.reshape(1, D), ln_b.reshape(1, D),
    )
    x_spec = pl.BlockSpec((None, TILE, D), lambda b, i: (b, i, 0))
    in_specs = [x_spec] + [_resident(a.shape) for a in args[1:]]
    n_chunks = TILE // CHUNK
    row_scratch = pltpu.VMEM((n_chunks, H, 1, CHUNK), jnp.float32)
    scratch = [
        pltpu.VMEM((TILE, D), wdt),
        pltpu.VMEM((2, TILE + CONV_HIST, 3 * dv), jnp.float32),
        pltpu.VMEM((TILE, dv), jnp.float32),
        pltpu.VMEM((TILE, dk), jnp.float32),
        pltpu.VMEM((dk, TILE), jnp.float32),
        pltpu.VMEM((TILE, dv), wdt),
        pltpu.VMEM((H, TILE, dv), wdt),
        row_scratch, row_scratch, row_scratch, row_scratch, row_scratch, row_scratch,
        pltpu.VMEM((H, CHUNK), jnp.float32),
        pltpu.VMEM((H, dk, dv), jnp.float32),
        pltpu.VMEM((H, 1, dk), jnp.float32),
        pltpu.VMEM((H, CONV_HIST, dv), jnp.float32),
    ]
    kern = functools.partial(_mlstm_layer_kernel, alpha=alpha, q_scale=float(dk) ** -0.5)
    return pl.pallas_call(
        kern,
        grid=(B, S // TILE),
        in_specs=in_specs,
        out_specs=pl.BlockSpec((None, TILE, D), lambda b, i: (b, i, 0)),
        out_shape=jax.ShapeDtypeStruct((B, S, D), x.dtype),
        scratch_shapes=scratch,
        compiler_params=pltpu.CompilerParams(
            dimension_semantics=("arbitrary", "arbitrary"), vmem_limit_bytes=VMEM_LIMIT_BYTES),
        name="mlstm_layer",
    )(*args)


def _pool_layer_kernel(x_ref, w_in_ref, wg_ref, sc_ref, wo_ref, lg_ref, lb_ref, o_ref,
                       xb_s, u_s, gt_s, hist_s, *, alpha):
    n_groups, _, cg = u_s.shape
    first_tile = pl.program_id(1) == 0

    @pl.when(first_tile)
    def _():
        hist_s[...] = jnp.zeros_like(hist_s)

    xb_s[...] = x_ref[...].astype(xb_s.dtype)
    head_row = lax.broadcasted_iota(jnp.int32, (POOL_HIST, cg), 0)

    for g, win in enumerate(POOL_WINDOWS):
        u = _dot(xb_s[...], w_in_ref[:, g * cg:(g + 1) * cg])
        z = _dot(xb_s[...], w_in_ref[:, n_groups * cg + g * cg:n_groups * cg + (g + 1) * cg])
        u_s[g, 0:POOL_HIST, :] = hist_s[g]
        u_s[g, POOL_HIST:, :] = u
        hist_s[g] = u_s[g, TILE:TILE + POOL_HIST, :]
        wsum = u_s[g]
        span = 1
        while span < win:
            wsum = wsum + pltpu.roll(wsum, span, 0)
            span *= 2
        wsum = wsum[POOL_HIST:]
        cnt = jnp.where(first_tile, jnp.minimum(head_row + 1, win), win).astype(jnp.float32)
        pooled = jnp.concatenate(
            [wsum[:POOL_HIST] / cnt - u[:POOL_HIST], wsum[POOL_HIST:] * (1.0 / win) - u[POOL_HIST:]],
            axis=0)
        mixed = _dot(pooled, wg_ref[g]) * sc_ref[g]
        gt_s[:, g * cg:(g + 1) * cg] = (mixed * _silu(z)).astype(gt_s.dtype)

    branch = _dot(gt_s[...], wo_ref[...])
    o_ref[...] = _post_norm(x_ref[...], branch, lg_ref[...], lb_ref[...], alpha)


def _pool_layer(x, w_in, w_group, scale, w_out, ln_g, ln_b, alpha):
    B, S, D = x.shape
    G, cg, _ = w_group.shape
    E = G * cg
    assert S % TILE == 0 and G == len(POOL_WINDOWS) and max(POOL_WINDOWS) <= POOL_HIST
    wdt = MXU_DTYPE
    args = (x, w_in.astype(wdt), w_group.astype(wdt), scale.reshape(G, 1, cg),
            w_out.astype(wdt), ln_g.reshape(1, D), ln_b.reshape(1, D))
    in_specs = ([pl.BlockSpec((None, TILE, D), lambda b, i: (b, i, 0))]
                + [_resident(a.shape) for a in args[1:]])
    scratch = [
        pltpu.VMEM((TILE, D), wdt),
        pltpu.VMEM((G, TILE + POOL_HIST, cg), jnp.float32),
        pltpu.VMEM((TILE, E), wdt),
        pltpu.VMEM((G, POOL_HIST, cg), jnp.float32),
    ]
    return pl.pallas_call(
        functools.partial(_pool_layer_kernel, alpha=alpha),
        grid=(B, S // TILE),
        in_specs=in_specs,
        out_specs=pl.BlockSpec((None, TILE, D), lambda b, i: (b, i, 0)),
        out_shape=jax.ShapeDtypeStruct((B, S, D), x.dtype),
        scratch_shapes=scratch,
        compiler_params=pltpu.CompilerParams(
            dimension_semantics=("arbitrary", "arbitrary"), vmem_limit_bytes=VMEM_LIMIT_BYTES),
        name="pool_layer",
    )(*args)


def kernel(x, mlstm_w_in, mlstm_conv_w, mlstm_conv_b, mlstm_w_q, mlstm_w_k, mlstm_w_v, mlstm_b_i,
           mlstm_b_f, mlstm_norm_g, mlstm_skip, mlstm_w_out, pool_w_in, pool_w_group, pool_scale,
           pool_w_out, post_ln_g, post_ln_b):
    depth = post_ln_g.shape[0]
    alpha = (2.0 * depth) ** 0.25
    for i in range(depth):
        j = i // 2
        if i % 2 == 0:
            x = _mlstm_layer(x, mlstm_w_in[j], mlstm_conv_w[j], mlstm_conv_b[j], mlstm_w_q[j],
                             mlstm_w_k[j], mlstm_w_v[j], mlstm_b_i[j], mlstm_b_f[j],
                             mlstm_norm_g[j], mlstm_skip[j], mlstm_w_out[j],
                             post_ln_g[i], post_ln_b[i], alpha)
        else:
            x = _pool_layer(x, pool_w_in[j], pool_w_group[j], pool_scale[j], pool_w_out[j],
                            post_ln_g[i], post_ln_b[i], alpha)
    return x
```

```python
import functools

import jax
import jax.numpy as jnp
from jax import lax
from jax.experimental import pallas as pl
from jax.experimental.pallas import tpu as pltpu

N_HEADS = 8
CONV_K = 4
CHUNK = 128
POOL_WINDOWS = (2, 4, 8, 16)
LN_EPS = 1e-5
TILE = 512
CONV_HIST = 8
POOL_HIST = 16
MXU_DTYPE = jnp.bfloat16
VMEM_LIMIT_BYTES = 52 * 1024 * 1024


NEG_LOG2_E = -1.4426950408889634


def _sigmoid(x):
    return 1.0 / (1.0 + jnp.exp2(x * NEG_LOG2_E))


def _silu(x):
    return x * _sigmoid(x)


def _log_sigmoid(x):
    return jnp.minimum(x, 0.0) - jnp.log1p(jnp.exp(-jnp.abs(x)))


def _dot(a, b):
    return jnp.dot(a.astype(MXU_DTYPE), b.astype(MXU_DTYPE), preferred_element_type=jnp.float32)


def _dot_nt(a, b):
    return lax.dot_general(a.astype(MXU_DTYPE), b.astype(MXU_DTYPE),
                           (((1,), (1,)), ((), ())), preferred_element_type=jnp.float32)


def _dot_f32_lhs(a, b):
    r = a.shape[0]
    hi = a.astype(jnp.bfloat16).astype(jnp.float32)
    rest = a - hi
    mid = rest.astype(jnp.bfloat16).astype(jnp.float32)
    parts = jnp.concatenate([hi, mid, rest - mid], axis=0).astype(jnp.bfloat16)
    out = jnp.dot(parts, b.astype(jnp.bfloat16), preferred_element_type=jnp.float32)
    return out[:r] + out[r:2 * r] + out[2 * r:]


def _lane_prefix_max(x):
    lane = lax.broadcasted_iota(jnp.int32, x.shape, 1)
    shift = 1
    while shift < x.shape[1]:
        x = jnp.maximum(x, jnp.where(lane >= shift, pltpu.roll(x, shift, 1), -jnp.inf))
        shift *= 2
    return x


def _column_of(row):
    n = row.shape[1]
    return jnp.broadcast_to(row, (n, n)).T


def _last_lane(x):
    return jnp.broadcast_to(x[:, x.shape[1] - 1:], x.shape)


def _row_of(col):
    n = col.shape[0]
    return jnp.broadcast_to(col, (n, n)).T[0:1]


def _post_norm(x, branch, g, b, alpha):
    y = alpha * x + branch
    mu = jnp.mean(y, axis=-1, keepdims=True)
    cen = y - mu
    var = jnp.mean(cen * cen, axis=-1, keepdims=True)
    return cen * lax.rsqrt(var + LN_EPS) * g + b


def _mlstm_layer_kernel(x_ref, w_in_ref, wg_ref, bg_ref, cw_ref, cb_ref, wq_ref, wkt_ref, wv_ref,
                        ng_ref, sk_ref, wo_ref, lg_ref, lb_ref, o_ref,
                        xb_s, g_s, acc_s, c_s, mx_s, en_s, ws_s, wc_s, mp_s,
                        m_st, cst_s, nst_s, hist_s, *head_bufs, alpha, q_scale):
    n_chunks = TILE // CHUNK
    dv = g_s.shape[2]
    bufs = (head_bufs[:len(head_bufs) // 2], head_bufs[len(head_bufs) // 2:])
    first_tile = pl.program_id(1) == 0

    @pl.when(first_tile)
    def _():
        m_st[...] = jnp.zeros_like(m_st)
        cst_s[...] = jnp.zeros_like(cst_s)
        nst_s[...] = jnp.zeros_like(nst_s)
        hist_s[...] = jnp.zeros_like(hist_s)

    xb_s[...] = x_ref[...].astype(xb_s.dtype)

    gates = _dot_nt(wg_ref[...], xb_s[...]) + bg_ref[:, 0:1]
    log_i = gates[:N_HEADS]
    log_f = _log_sigmoid(gates[N_HEADS:])
    row_id = lax.broadcasted_iota(jnp.int32, (CHUNK, CHUNK), 0)
    col_id = lax.broadcasted_iota(jnp.int32, (CHUNK, CHUNK), 1)
    tri = row_id >= col_id
    lf_rows = jnp.concatenate([log_f[:, c * CHUNK:(c + 1) * CHUNK] for c in range(n_chunks)], axis=0)
    bcum_rows = _dot_f32_lhs(lf_rows, jnp.where(row_id <= col_id, 1.0, 0.0))
    li_rows = jnp.concatenate([log_i[:, c * CHUNK:(c + 1) * CHUNK] for c in range(n_chunks)], axis=0)
    cc_rows = li_rows - bcum_rows
    cmax_rows = _lane_prefix_max(cc_rows)
    m_prev = m_st[...]
    for c in range(n_chunks):
        blk = slice(c * N_HEADS, (c + 1) * N_HEADS)
        bcum = bcum_rows[blk]
        cc = cc_rows[blk]
        mx = jnp.maximum(m_prev, cmax_rows[blk])
        mx_last = _last_lane(mx)
        rows = (cc, mx, jnp.exp(-(bcum + mx)), jnp.exp(cc - mx_last), jnp.exp(m_prev - mx_last), m_prev)
        for dst, val in zip((c_s, mx_s, en_s, ws_s, wc_s, mp_s), rows):
            for hh in range(N_HEADS):
                dst[c, hh] = val[hh:hh + 1]
        m_prev = _last_lane(bcum) + mx_last
    m_st[...] = m_prev

    def project(h, slot, part):
        dst = bufs[slot][part]
        start = part * N_HEADS * dv + h * dv
        out = _dot(xb_s[...], w_in_ref[:, start:start + dv])
        if part == 0:
            dst[0:CONV_HIST, :] = hist_s[h]
            dst[CONV_HIST:, :] = out
            hist_s[h] = dst[TILE:TILE + CONV_HIST, :]
        else:
            dst[...] = out

    def mix(h, slot, fillers):
        xm_s, z_s, o_s, xc_s, q_s, kt_s, v_s = bufs[slot]
        cw = cw_ref[h]
        xe = xm_s[...]
        xe1 = pltpu.roll(xe, 1, 0)
        pair = cw[1:2] * xe + cw[0:1] * xe1
        conv = cb_ref[h] + cw[3:4] * xe + cw[2:3] * xe1 + pltpu.roll(pair, 2, 0)
        xc = _silu(conv[CONV_HIST:])
        xc_s[...] = xc
        xc_b = xc.astype(MXU_DTYPE)
        q_s[...] = _dot(xc_b, wq_ref[h]) * q_scale
        kt_s[...] = _dot_nt(wkt_ref[h], xc_b)
        v_s[...] = _dot(xm_s[CONV_HIST:, :], wv_ref[h]).astype(v_s.dtype)

        for c in range(n_chunks):
            rows = slice(c * CHUNK, (c + 1) * CHUNK)
            m_in = mp_s[c, h][:, 0:1]
            mx_col = _column_of(mx_s[c, h])
            decay = jnp.where(tri, jnp.exp(c_s[c, h] - mx_col), 0.0)
            qc = q_s[rows, :]
            ktc = kt_s[:, rows]
            vc = v_s[rows, :]
            scores = _dot(qc, ktc)
            kts = ktc * ws_s[c, h]
            kv = _dot(kts, vc)
            if fillers[c] is not None:
                fillers[c]()
            s = scores * decay
            qs = qc * jnp.exp(m_in - mx_col)
            c_prev = cst_s[h]
            n_prev = nst_s[h]
            den = jnp.sum(s + qs * n_prev, axis=-1, keepdims=True)
            num = _dot(jnp.concatenate([s, qs], axis=1),
                       jnp.concatenate([vc, c_prev.astype(MXU_DTYPE)], axis=0))
            hv = num / jnp.maximum(jnp.abs(den), _column_of(en_s[c, h])[:, 0:1])
            wc = wc_s[c, h][:, 0:1]
            cst_s[h] = wc * c_prev + kv
            nst_s[h] = wc * n_prev + _row_of(jnp.sum(kts, axis=-1, keepdims=True))
            mu = jnp.mean(hv, axis=-1, keepdims=True)
            cen = hv - mu
            var = jnp.mean(cen * cen, axis=-1, keepdims=True)
            hn = cen * lax.rsqrt(var + LN_EPS) * ng_ref[h]
            hb = _sigmoid(o_s[rows, :]) * hn + sk_ref[h] * xc_s[rows, :]
            g_s[h, rows, :] = (hb * _silu(z_s[rows, :])).astype(g_s.dtype)

    half = acc_s.shape[1] // 2

    def out_project(h, part):
        cols = slice(part * half, (part + 1) * half)
        contrib = _dot(g_s[h], wo_ref[h, :, cols])
        if h == 0:
            acc_s[:, cols] = contrib
        else:
            acc_s[:, cols] += contrib

    for part in range(3):
        project(0, 0, part)
    for h in range(N_HEADS):
        slot = h % 2
        fillers = [None] * n_chunks
        if h + 1 < N_HEADS:
            project(h + 1, 1 - slot, 0)
            fillers[0] = functools.partial(project, h + 1, 1 - slot, 1)
            fillers[1] = functools.partial(project, h + 1, 1 - slot, 2)
        if h > 0:
            fillers[2] = functools.partial(out_project, h - 1, 0)
            fillers[3] = functools.partial(out_project, h - 1, 1)
        mix(h, slot, fillers)
    branch = acc_s[...] + _dot(g_s[N_HEADS - 1], wo_ref[N_HEADS - 1])
    o_ref[...] = _post_norm(x_ref[...], branch, lg_ref[...], lb_ref[...], alpha)


def _resident(shape):
    zeros = (0,) * len(shape)
    return pl.BlockSpec(shape, lambda b, i: zeros, pipeline_mode=pl.Buffered(1))


def _mlstm_layer(x, w_in, conv_w, conv_b, w_q, w_k, w_v, b_i, b_f, norm_g, skip, w_out,
                 ln_g, ln_b, alpha):
    B, S, D = x.shape
    H = N_HEADS
    dv = w_v.shape[-1]
    dk = w_q.shape[-1]
    E = H * dv
    assert S % TILE == 0 and TILE % CHUNK == 0 and CHUNK == dk and H % 2 == 0
    assert conv_w.shape[0] == CONV_K == 4 and CONV_HIST >= CONV_K - 1
    wdt = MXU_DTYPE
    w_main = w_in[:, :3 * E].astype(wdt)
    w_gate = w_in[:, 3 * E:].T.astype(wdt)
    b_gate = jnp.broadcast_to(jnp.concatenate([b_i, b_f])[:, None], (2 * H, 128)).astype(jnp.float32)
    args = (
        x, w_main, w_gate, b_gate,
        conv_w.reshape(CONV_K, H, dv).transpose(1, 0, 2), conv_b.reshape(H, 1, dv),
        w_q.astype(wdt), w_k.transpose(0, 2, 1).astype(wdt), w_v.astype(wdt),
        norm_g.reshape(H, 1, dv), skip.reshape(H, 1, dv),
        w_out.reshape(H, dv, D).astype(wdt), ln_g.reshape(1, D), ln_b.reshape(1, D),
    )
    x_spec = pl.BlockSpec((None, TILE, D), lambda b, i: (b, i, 0))
    in_specs = [x_spec] + [_resident(a.shape) for a in args[1:]]
    n_chunks = TILE // CHUNK
    row_scratch = pltpu.VMEM((n_chunks, H, 1, CHUNK), jnp.float32)
    head_bufs = [
        pltpu.VMEM((TILE + CONV_HIST, dv), jnp.float32),
        pltpu.VMEM((TILE, dv), jnp.float32),
        pltpu.VMEM((TILE, dv), jnp.float32),
        pltpu.VMEM((TILE, dv), jnp.float32),
        pltpu.VMEM((TILE, dk), jnp.float32),
        pltpu.VMEM((dk, TILE), jnp.float32),
        pltpu.VMEM((TILE, dv), wdt),
    ]
    scratch = [
        pltpu.VMEM((TILE, D), wdt),
        pltpu.VMEM((H, TILE, dv), wdt),
        pltpu.VMEM((TILE, D), jnp.float32),
        row_scratch, row_scratch, row_scratch, row_scratch, row_scratch, row_scratch,
        pltpu.VMEM((H, CHUNK), jnp.float32),
        pltpu.VMEM((H, dk, dv), jnp.float32),
        pltpu.VMEM((H, 1, dk), jnp.float32),
        pltpu.VMEM((H, CONV_HIST, dv), jnp.float32),
    ] + head_bufs + head_bufs
    kern = functools.partial(_mlstm_layer_kernel, alpha=alpha, q_scale=float(dk) ** -0.5)
    return pl.pallas_call(
        kern,
        grid=(B, S // TILE),
        in_specs=in_specs,
        out_specs=pl.BlockSpec((None, TILE, D), lambda b, i: (b, i, 0)),
        out_shape=jax.ShapeDtypeStruct((B, S, D), x.dtype),
        scratch_shapes=scratch,
        compiler_params=pltpu.CompilerParams(
            dimension_semantics=("arbitrary", "arbitrary"), vmem_limit_bytes=VMEM_LIMIT_BYTES),
        name="mlstm_layer",
    )(*args)


def _pool_layer_kernel(x_ref, w_in_ref, wg_ref, sc_ref, wo_ref, lg_ref, lb_ref, o_ref,
                       xb_s, u_s, gt_s, hist_s, *, alpha):
    n_groups, _, cg = u_s.shape
    first_tile = pl.program_id(1) == 0

    @pl.when(first_tile)
    def _():
        hist_s[...] = jnp.zeros_like(hist_s)

    xb_s[...] = x_ref[...].astype(xb_s.dtype)
    head_row = lax.broadcasted_iota(jnp.int32, (POOL_HIST, cg), 0)

    for g, win in enumerate(POOL_WINDOWS):
        u = _dot(xb_s[...], w_in_ref[:, g * cg:(g + 1) * cg])
        z = _dot(xb_s[...], w_in_ref[:, n_groups * cg + g * cg:n_groups * cg + (g + 1) * cg])
        u_s[g, 0:POOL_HIST, :] = hist_s[g]
        u_s[g, POOL_HIST:, :] = u
        hist_s[g] = u_s[g, TILE:TILE + POOL_HIST, :]
        wsum = u_s[g]
        span = 1
        while span < win:
            wsum = wsum + pltpu.roll(wsum, span, 0)
            span *= 2
        wsum = wsum[POOL_HIST:]
        cnt = jnp.where(first_tile, jnp.minimum(head_row + 1, win), win).astype(jnp.float32)
        pooled = jnp.concatenate(
            [wsum[:POOL_HIST] / cnt - u[:POOL_HIST], wsum[POOL_HIST:] * (1.0 / win) - u[POOL_HIST:]],
            axis=0)
        mixed = _dot(pooled, wg_ref[g]) * sc_ref[g]
        gt_s[:, g * cg:(g + 1) * cg] = (mixed * _silu(z)).astype(gt_s.dtype)

    branch = _dot(gt_s[...], wo_ref[...])
    o_ref[...] = _post_norm(x_ref[...], branch, lg_ref[...], lb_ref[...], alpha)


def _pool_layer(x, w_in, w_group, scale, w_out, ln_g, ln_b, alpha):
    B, S, D = x.shape
    G, cg, _ = w_group.shape
    E = G * cg
    assert S % TILE == 0 and G == len(POOL_WINDOWS) and max(POOL_WINDOWS) <= POOL_HIST
    wdt = MXU_DTYPE
    args = (x, w_in.astype(wdt), w_group.astype(wdt), scale.reshape(G, 1, cg),
            w_out.astype(wdt), ln_g.reshape(1, D), ln_b.reshape(1, D))
    in_specs = ([pl.BlockSpec((None, TILE, D), lambda b, i: (b, i, 0))]
                + [_resident(a.shape) for a in args[1:]])
    scratch = [
        pltpu.VMEM((TILE, D), wdt),
        pltpu.VMEM((G, TILE + POOL_HIST, cg), jnp.float32),
        pltpu.VMEM((TILE, E), wdt),
        pltpu.VMEM((G, POOL_HIST, cg), jnp.float32),
    ]
    return pl.pallas_call(
        functools.partial(_pool_layer_kernel, alpha=alpha),
        grid=(B, S // TILE),
        in_specs=in_specs,
        out_specs=pl.BlockSpec((None, TILE, D), lambda b, i: (b, i, 0)),
        out_shape=jax.ShapeDtypeStruct((B, S, D), x.dtype),
        scratch_shapes=scratch,
        compiler_params=pltpu.CompilerParams(
            dimension_semantics=("arbitrary", "arbitrary"), vmem_limit_bytes=VMEM_LIMIT_BYTES),
        name="pool_layer",
    )(*args)


def kernel(x, mlstm_w_in, mlstm_conv_w, mlstm_conv_b, mlstm_w_q, mlstm_w_k, mlstm_w_v, mlstm_b_i,
           mlstm_b_f, mlstm_norm_g, mlstm_skip, mlstm_w_out, pool_w_in, pool_w_group, pool_scale,
           pool_w_out, post_ln_g, post_ln_b):
    depth = post_ln_g.shape[0]
    alpha = (2.0 * depth) ** 0.25
    for i in range(depth):
        j = i // 2
        if i % 2 == 0:
            x = _mlstm_layer(x, mlstm_w_in[j], mlstm_conv_w[j], mlstm_conv_b[j], mlstm_w_q[j],
                             mlstm_w_k[j], mlstm_w_v[j], mlstm_b_i[j], mlstm_b_f[j],
                             mlstm_norm_g[j], mlstm_skip[j], mlstm_w_out[j],
                             post_ln_g[i], post_ln_b[i], alpha)
        else:
            x = _pool_layer(x, pool_w_in[j], pool_w_group[j], pool_scale[j], pool_w_out[j],
                            post_ln_g[i], post_ln_b[i], alpha)
    return x
```

```python
import functools

import jax
import jax.numpy as jnp
from jax import lax
from jax.experimental import pallas as pl
from jax.experimental.pallas import tpu as pltpu

N_HEADS = 8
CONV_K = 4
CHUNK = 128
POOL_WINDOWS = (2, 4, 8, 16)
LN_EPS = 1e-5
TILE = 512
CONV_HIST = 8
POOL_HIST = 16
MXU_DTYPE = jnp.bfloat16
VMEM_LIMIT_BYTES = 52 * 1024 * 1024


NEG_LOG2_E = -1.4426950408889634


def _sigmoid(x):
    return 1.0 / (1.0 + jnp.exp2(x * NEG_LOG2_E))


def _silu(x):
    return x * _sigmoid(x)


def _log_sigmoid(x):
    return jnp.minimum(x, 0.0) - jnp.log1p(jnp.exp(-jnp.abs(x)))


def _dot(a, b):
    return jnp.dot(a.astype(MXU_DTYPE), b.astype(MXU_DTYPE), preferred_element_type=jnp.float32)


def _dot_nt(a, b):
    return lax.dot_general(a.astype(MXU_DTYPE), b.astype(MXU_DTYPE),
                           (((1,), (1,)), ((), ())), preferred_element_type=jnp.float32)


def _dot_f32_lhs(a, b):
    r = a.shape[0]
    hi = a.astype(jnp.bfloat16).astype(jnp.float32)
    rest = a - hi
    mid = rest.astype(jnp.bfloat16).astype(jnp.float32)
    parts = jnp.concatenate([hi, mid, rest - mid], axis=0).astype(jnp.bfloat16)
    out = jnp.dot(parts, b.astype(jnp.bfloat16), preferred_element_type=jnp.float32)
    return out[:r] + out[r:2 * r] + out[2 * r:]


def _lane_prefix_max(x):
    lane = lax.broadcasted_iota(jnp.int32, x.shape, 1)
    shift = 1
    while shift < x.shape[1]:
        x = jnp.maximum(x, jnp.where(lane >= shift, pltpu.roll(x, shift, 1), -jnp.inf))
        shift *= 2
    return x


def _column_of(row):
    n = row.shape[1]
    return jnp.broadcast_to(row, (n, n)).T


def _last_lane(x):
    return jnp.broadcast_to(x[:, x.shape[1] - 1:], x.shape)


def _row_of(col):
    n = col.shape[0]
    return jnp.broadcast_to(col, (n, n)).T[0:1]


def _layer_norm(y, g, b):
    mu = jnp.mean(y, axis=-1, keepdims=True)
    cen = y - mu
    var = jnp.mean(cen * cen, axis=-1, keepdims=True)
    return cen * lax.rsqrt(var + LN_EPS) * g + b


NORM_PARTS = 4


def _norm_part(part, y_s, o_ref, lg_ref, lb_ref):
    n_rows = y_s.shape[0] // NORM_PARTS
    rows = slice(part * n_rows, (part + 1) * n_rows)
    out = _layer_norm(y_s[rows, :], lg_ref[...], lb_ref[...])
    o_ref[rows, :] = out
    folded = out[:, 0:128]
    for j in range(1, out.shape[1] // 128):
        folded = folded + out[:, j * 128:(j + 1) * 128]
    folded = folded.reshape(n_rows // 8, 8, 128).sum(axis=0)
    bits = pltpu.bitcast(folded, jnp.uint32)
    for _ in range(2):
        bits = lax.shift_right_logical(bits, jnp.uint32(16))
    return pltpu.bitcast(bits, jnp.float32)


def _with_deferred_norm(tile_fn, tiles_per_seq, y_s, o_ref, lg_ref, lb_ref):
    step = pl.program_id(0)
    last = pl.num_programs(0) - 1

    @pl.when(step == 0)
    def _():
        y_s[...] = jnp.zeros_like(y_s)

    @pl.when(step < last)
    def _():
        tile_fn(step % tiles_per_seq == 0)

    @pl.when(step == last)
    def _():
        o_ref[...] = _layer_norm(y_s[...], lg_ref[...], lb_ref[...])


def _deferred_specs(n_batch, tiles_per_seq, d_model):
    n_tiles = n_batch * tiles_per_seq

    def x_map(s):
        t = jnp.minimum(s, n_tiles - 1)
        return (t // tiles_per_seq, t % tiles_per_seq, 0)

    def o_map(s):
        t = jnp.maximum(s - 1, 0)
        return (t // tiles_per_seq, t % tiles_per_seq, 0)

    return (pl.BlockSpec((None, TILE, d_model), x_map), pl.BlockSpec((None, TILE, d_model), o_map),
            (n_tiles + 1,))


def _mlstm_layer_kernel(x_ref, w_in_ref, wg_ref, bg_ref, cw_ref, cb_ref, wq_ref, wkt_ref, wv_ref,
                        ng_ref, sk_ref, wo_ref, lg_ref, lb_ref, o_ref,
                        xb_s, g_s, acc_s, c_s, mx_s, en_s, ws_s, wc_s, mp_s,
                        m_st, cst_s, nst_s, hist_s, y_s, *head_bufs, tiles_per_seq, **consts):
    tile_fn = functools.partial(
        _mlstm_tile, x_ref, w_in_ref, wg_ref, bg_ref, cw_ref, cb_ref, wq_ref, wkt_ref, wv_ref,
        ng_ref, sk_ref, wo_ref, lg_ref, lb_ref, o_ref, xb_s, g_s, acc_s, c_s, mx_s, en_s, ws_s, wc_s,
        mp_s, m_st, cst_s, nst_s, hist_s, y_s, head_bufs, **consts)
    _with_deferred_norm(tile_fn, tiles_per_seq, y_s, o_ref, lg_ref, lb_ref)


def _mlstm_tile(x_ref, w_in_ref, wg_ref, bg_ref, cw_ref, cb_ref, wq_ref, wkt_ref, wv_ref,
                ng_ref, sk_ref, wo_ref, lg_ref, lb_ref, o_ref,
                xb_s, g_s, acc_s, c_s, mx_s, en_s, ws_s, wc_s, mp_s,
                m_st, cst_s, nst_s, hist_s, y_s, head_bufs, first_tile, *, alpha, q_scale):
    n_chunks = TILE // CHUNK
    dv = g_s.shape[2]
    bufs = (head_bufs[:len(head_bufs) // 2], head_bufs[len(head_bufs) // 2:])

    @pl.when(first_tile)
    def _():
        m_st[...] = jnp.zeros_like(m_st)
        cst_s[...] = jnp.zeros_like(cst_s)
        nst_s[...] = jnp.zeros_like(nst_s)
        hist_s[...] = jnp.zeros_like(hist_s)

    xb_s[...] = x_ref[...].astype(xb_s.dtype)

    gates = _dot_nt(wg_ref[...], xb_s[...]) + bg_ref[:, 0:1]
    norm_done = [_norm_part(part, y_s, o_ref, lg_ref, lb_ref) for part in range(NORM_PARTS)]
    log_i = gates[:N_HEADS]
    log_f = _log_sigmoid(gates[N_HEADS:])
    row_id = lax.broadcasted_iota(jnp.int32, (CHUNK, CHUNK), 0)
    col_id = lax.broadcasted_iota(jnp.int32, (CHUNK, CHUNK), 1)
    tri = row_id >= col_id
    lf_rows = jnp.concatenate([log_f[:, c * CHUNK:(c + 1) * CHUNK] for c in range(n_chunks)], axis=0)
    bcum_rows = _dot_f32_lhs(lf_rows, jnp.where(row_id <= col_id, 1.0, 0.0))
    li_rows = jnp.concatenate([log_i[:, c * CHUNK:(c + 1) * CHUNK] for c in range(n_chunks)], axis=0)
    cc_rows = li_rows - bcum_rows
    cmax_rows = _lane_prefix_max(cc_rows)
    m_prev = m_st[...]
    for c in range(n_chunks):
        blk = slice(c * N_HEADS, (c + 1) * N_HEADS)
        bcum = bcum_rows[blk]
        cc = cc_rows[blk]
        mx = jnp.maximum(m_prev, cmax_rows[blk])
        mx_last = _last_lane(mx)
        rows = (cc, mx, jnp.exp(-(bcum + mx)), jnp.exp(cc - mx_last), jnp.exp(m_prev - mx_last), m_prev)
        for dst, val in zip((c_s, mx_s, en_s, ws_s, wc_s, mp_s), rows):
            for hh in range(N_HEADS):
                dst[c, hh] = val[hh:hh + 1]
        m_prev = _last_lane(bcum) + mx_last
    m_st[...] = m_prev

    def project(h, slot, part):
        dst = bufs[slot][part]
        start = part * N_HEADS * dv + h * dv
        out = _dot(xb_s[...], w_in_ref[:, start:start + dv])
        if part == 0:
            history = hist_s[h]
            if h < NORM_PARTS:
                history = history + norm_done[h][:, 0:1]
            dst[0:CONV_HIST, :] = history
            dst[CONV_HIST:, :] = out
            hist_s[h] = dst[TILE:TILE + CONV_HIST, :]
        else:
            dst[...] = out

    def mix(h, slot, fillers):
        xm_s, z_s, o_s, xc_s, q_s, kt_s, v_s = bufs[slot]
        cw = cw_ref[h]
        xe = xm_s[...]
        xe1 = pltpu.roll(xe, 1, 0)
        pair = cw[1:2] * xe + cw[0:1] * xe1
        conv = cb_ref[h] + cw[3:4] * xe + cw[2:3] * xe1 + pltpu.roll(pair, 2, 0)
        xc = _silu(conv[CONV_HIST:])
        xc_s[...] = xc
        xc_b = xc.astype(MXU_DTYPE)
        q_s[...] = _dot(xc_b, wq_ref[h]) * q_scale
        kt_s[...] = _dot_nt(wkt_ref[h], xc_b)
        v_s[...] = _dot(xm_s[CONV_HIST:, :], wv_ref[h]).astype(v_s.dtype)

        for c in range(n_chunks):
            rows = slice(c * CHUNK, (c + 1) * CHUNK)
            m_in = mp_s[c, h][:, 0:1]
            mx_col = _column_of(mx_s[c, h])
            decay = jnp.where(tri, jnp.exp(c_s[c, h] - mx_col), 0.0)
            qc = q_s[rows, :]
            ktc = kt_s[:, rows]
            vc = v_s[rows, :]
            scores = _dot(qc, ktc)
            kts = ktc * ws_s[c, h]
            kv = _dot(kts, vc)
            if fillers[c] is not None:
                fillers[c]()
            s = scores * decay
            qs = qc * jnp.exp(m_in - mx_col)
            c_prev = cst_s[h]
            n_prev = nst_s[h]
            den = jnp.sum(s + qs * n_prev, axis=-1, keepdims=True)
            num = _dot(jnp.concatenate([s, qs], axis=1),
                       jnp.concatenate([vc, c_prev.astype(MXU_DTYPE)], axis=0))
            hv = num / jnp.maximum(jnp.abs(den), _column_of(en_s[c, h])[:, 0:1])
            wc = wc_s[c, h][:, 0:1]
            cst_s[h] = wc * c_prev + kv
            nst_s[h] = wc * n_prev + _row_of(jnp.sum(kts, axis=-1, keepdims=True))
            mu = jnp.mean(hv, axis=-1, keepdims=True)
            cen = hv - mu
            var = jnp.mean(cen * cen, axis=-1, keepdims=True)
            hn = cen * lax.rsqrt(var + LN_EPS) * ng_ref[h]
            hb = _sigmoid(o_s[rows, :]) * hn + sk_ref[h] * xc_s[rows, :]
            g_s[h, rows, :] = (hb * _silu(z_s[rows, :])).astype(g_s.dtype)

    half = acc_s.shape[1] // 2

    def out_project(h, part):
        cols = slice(part * half, (part + 1) * half)
        contrib = _dot(g_s[h], wo_ref[h, :, cols])
        if h == 0:
            acc_s[:, cols] = contrib
        else:
            acc_s[:, cols] += contrib

    for part in range(3):
        project(0, 0, part)
    for h in range(N_HEADS):
        slot = h % 2
        fillers = [None] * n_chunks
        if h + 1 < N_HEADS:
            project(h + 1, 1 - slot, 0)
            fillers[0] = functools.partial(project, h + 1, 1 - slot, 1)
            fillers[1] = functools.partial(project, h + 1, 1 - slot, 2)
        if h > 0:
            fillers[2] = functools.partial(out_project, h - 1, 0)
            fillers[3] = functools.partial(out_project, h - 1, 1)
        mix(h, slot, fillers)
    y_s[...] = alpha * x_ref[...] + (acc_s[...] + _dot(g_s[N_HEADS - 1], wo_ref[N_HEADS - 1]))


def _resident(arr, layer=None, block=None):
    if layer is None:
        index = (0,) * arr.ndim
        return pl.BlockSpec(arr.shape, lambda s: index, pipeline_mode=pl.Buffered(1))
    shape = tuple(arr.shape[1:]) if block is None else tuple(block)
    index = (layer,) + (0,) * len(shape)
    return pl.BlockSpec((None,) + shape, lambda s: index, pipeline_mode=pl.Buffered(1))


def _mlstm_layer(x, layer, stacks, w_in, conv_w, conv_b, b_i, b_f, norm_g, skip, ln_g, ln_b, alpha):
    B, S, D = x.shape
    H = N_HEADS
    w_in_all, wq_all, wkt_all, wv_all, wo_all = stacks
    dv = wv_all.shape[-1]
    dk = wq_all.shape[-1]
    E = H * dv
    assert S % TILE == 0 and TILE % CHUNK == 0 and CHUNK == dk and H % 2 == 0
    assert conv_w.shape[0] == CONV_K == 4 and CONV_HIST >= CONV_K - 1 and NORM_PARTS <= H
    wdt = MXU_DTYPE
    w_gate = w_in[:, 3 * E:].T.astype(wdt)
    b_gate = jnp.broadcast_to(jnp.concatenate([b_i, b_f])[:, None], (2 * H, 128)).astype(jnp.float32)
    small = (
        w_gate, b_gate, conv_w.reshape(CONV_K, H, dv).transpose(1, 0, 2), conv_b.reshape(H, 1, dv),
        norm_g.reshape(H, 1, dv), skip.reshape(H, 1, dv), ln_g.reshape(1, D), ln_b.reshape(1, D),
    )
    w_gate, b_gate, conv_w, conv_b, norm_g, skip, ln_g, ln_b = small
    args = (x, w_in_all, w_gate, b_gate, conv_w, conv_b, wq_all, wkt_all, wv_all, norm_g, skip, wo_all,
            ln_g, ln_b)
    x_spec, out_spec, grid = _deferred_specs(B, S // TILE, D)
    in_specs = [
        x_spec, _resident(w_in_all, layer, (D, 3 * E)),
        _resident(w_gate), _resident(b_gate), _resident(conv_w), _resident(conv_b),
        _resident(wq_all, layer), _resident(wkt_all, layer), _resident(wv_all, layer),
        _resident(norm_g), _resident(skip), _resident(wo_all, layer), _resident(ln_g), _resident(ln_b),
    ]
    n_chunks = TILE // CHUNK
    row_scratch = pltpu.VMEM((n_chunks, H, 1, CHUNK), jnp.float32)
    head_bufs = [
        pltpu.VMEM((TILE + CONV_HIST, dv), jnp.float32),
        pltpu.VMEM((TILE, dv), jnp.float32),
        pltpu.VMEM((TILE, dv), jnp.float32),
        pltpu.VMEM((TILE, dv), jnp.float32),
        pltpu.VMEM((TILE, dk), jnp.float32),
        pltpu.VMEM((dk, TILE), jnp.float32),
        pltpu.VMEM((TILE, dv), wdt),
    ]
    scratch = [
        pltpu.VMEM((TILE, D), wdt),
        pltpu.VMEM((H, TILE, dv), wdt),
        pltpu.VMEM((TILE, D), jnp.float32),
        row_scratch, row_scratch, row_scratch, row_scratch, row_scratch, row_scratch,
        pltpu.VMEM((H, CHUNK), jnp.float32),
        pltpu.VMEM((H, dk, dv), jnp.float32),
        pltpu.VMEM((H, 1, dk), jnp.float32),
        pltpu.VMEM((H, CONV_HIST, dv), jnp.float32),
        pltpu.VMEM((TILE, D), jnp.float32),
    ] + head_bufs + head_bufs
    kern = functools.partial(_mlstm_layer_kernel, tiles_per_seq=S // TILE, alpha=alpha,
                             q_scale=float(dk) ** -0.5)
    return pl.pallas_call(
        kern,
        grid=grid,
        in_specs=in_specs,
        out_specs=out_spec,
        out_shape=jax.ShapeDtypeStruct((B, S, D), x.dtype),
        scratch_shapes=scratch,
        compiler_params=pltpu.CompilerParams(
            dimension_semantics=("arbitrary",), vmem_limit_bytes=VMEM_LIMIT_BYTES),
        name="mlstm_layer",
    )(*args)


def _pool_layer_kernel(x_ref, w_in_ref, wg_ref, sc_ref, wo_ref, lg_ref, lb_ref, o_ref,
                       xb_s, u_s, gt_s, hist_s, y_s, *, tiles_per_seq, alpha):
    tile_fn = functools.partial(_pool_tile, x_ref, w_in_ref, wg_ref, sc_ref, wo_ref, lg_ref, lb_ref, o_ref,
                                xb_s, u_s, gt_s, hist_s, y_s, alpha)
    _with_deferred_norm(tile_fn, tiles_per_seq, y_s, o_ref, lg_ref, lb_ref)


def _pool_tile(x_ref, w_in_ref, wg_ref, sc_ref, wo_ref, lg_ref, lb_ref, o_ref,
               xb_s, u_s, gt_s, hist_s, y_s, alpha, first_tile):
    n_groups, _, cg = u_s.shape

    @pl.when(first_tile)
    def _():
        hist_s[...] = jnp.zeros_like(hist_s)

    xb_s[...] = x_ref[...].astype(xb_s.dtype)
    head_row = lax.broadcasted_iota(jnp.int32, (POOL_HIST, cg), 0)

    for g, win in enumerate(POOL_WINDOWS):
        u = _dot(xb_s[...], w_in_ref[:, g * cg:(g + 1) * cg])
        z = _dot(xb_s[...], w_in_ref[:, n_groups * cg + g * cg:n_groups * cg + (g + 1) * cg])
        u_s[g, 0:POOL_HIST, :] = hist_s[g] + _norm_part(g, y_s, o_ref, lg_ref, lb_ref)[0:1, 0:1]
        u_s[g, POOL_HIST:, :] = u
        hist_s[g] = u_s[g, TILE:TILE + POOL_HIST, :]
        wsum = u_s[g]
        span = 1
        while span < win:
            wsum = wsum + pltpu.roll(wsum, span, 0)
            span *= 2
        wsum = wsum[POOL_HIST:]
        cnt = jnp.where(first_tile, jnp.minimum(head_row + 1, win), win).astype(jnp.float32)
        pooled = jnp.concatenate(
            [wsum[:POOL_HIST] / cnt - u[:POOL_HIST], wsum[POOL_HIST:] * (1.0 / win) - u[POOL_HIST:]],
            axis=0)
        mixed = _dot(pooled, wg_ref[g]) * sc_ref[g]
        gt_s[:, g * cg:(g + 1) * cg] = (mixed * _silu(z)).astype(gt_s.dtype)

    y_s[...] = alpha * x_ref[...] + _dot(gt_s[...], wo_ref[...])


def _pool_layer(x, layer, stacks, scale, ln_g, ln_b, alpha):
    B, S, D = x.shape
    w_in_all, wg_all, wo_all = stacks
    G, cg = wg_all.shape[1:3]
    E = G * cg
    assert S % TILE == 0 and G == len(POOL_WINDOWS) == NORM_PARTS and max(POOL_WINDOWS) <= POOL_HIST
    wdt = MXU_DTYPE
    scale, ln_g, ln_b = scale.reshape(G, 1, cg), ln_g.reshape(1, D), ln_b.reshape(1, D)
    args = (x, w_in_all, wg_all, scale, wo_all, ln_g, ln_b)
    x_spec, out_spec, grid = _deferred_specs(B, S // TILE, D)
    in_specs = [x_spec, _resident(w_in_all, layer), _resident(wg_all, layer), _resident(scale),
                _resident(wo_all, layer), _resident(ln_g), _resident(ln_b)]
    scratch = [
        pltpu.VMEM((TILE, D), wdt),
        pltpu.VMEM((G, TILE + POOL_HIST, cg), jnp.float32),
        pltpu.VMEM((TILE, E), wdt),
        pltpu.VMEM((G, POOL_HIST, cg), jnp.float32),
        pltpu.VMEM((TILE, D), jnp.float32),
    ]
    return pl.pallas_call(
        functools.partial(_pool_layer_kernel, tiles_per_seq=S // TILE, alpha=alpha),
        grid=grid,
        in_specs=in_specs,
        out_specs=out_spec,
        out_shape=jax.ShapeDtypeStruct((B, S, D), x.dtype),
        scratch_shapes=scratch,
        compiler_params=pltpu.CompilerParams(
            dimension_semantics=("arbitrary",), vmem_limit_bytes=VMEM_LIMIT_BYTES),
        name="pool_layer",
    )(*args)


def kernel(x, mlstm_w_in, mlstm_conv_w, mlstm_conv_b, mlstm_w_q, mlstm_w_k, mlstm_w_v, mlstm_b_i,
           mlstm_b_f, mlstm_norm_g, mlstm_skip, mlstm_w_out, pool_w_in, pool_w_group, pool_scale,
           pool_w_out, post_ln_g, post_ln_b):
    depth = post_ln_g.shape[0]
    alpha = (2.0 * depth) ** 0.25
    wdt = MXU_DTYPE
    n_a, n_heads, dv, _ = mlstm_w_q.shape
    mlstm_stacks = (mlstm_w_in.astype(wdt), mlstm_w_q.astype(wdt),
                    mlstm_w_k.transpose(0, 1, 3, 2).astype(wdt), mlstm_w_v.astype(wdt),
                    mlstm_w_out.reshape(n_a, n_heads, dv, -1).astype(wdt))
    pool_stacks = (pool_w_in.astype(wdt), pool_w_group.astype(wdt), pool_w_out.astype(wdt))
    for i in range(depth):
        j = i // 2
        if i % 2 == 0:
            x = _mlstm_layer(x, j, mlstm_stacks, mlstm_w_in[j], mlstm_conv_w[j], mlstm_conv_b[j],
                             mlstm_b_i[j], mlstm_b_f[j], mlstm_norm_g[j], mlstm_skip[j],
                             post_ln_g[i], post_ln_b[i], alpha)
        else:
            x = _pool_layer(x, j, pool_stacks, pool_scale[j], post_ln_g[i], post_ln_b[i], alpha)
    return x
```

```python
import functools

import jax
import jax.numpy as jnp
from jax import lax
from jax.experimental import pallas as pl
from jax.experimental.pallas import tpu as pltpu

N_HEADS = 8
CONV_K = 4
CHUNK = 128
POOL_WINDOWS = (2, 4, 8, 16)
LN_EPS = 1e-5
TILE = 512
POOL_TILE = 1024
CONV_HIST = 8
POOL_HIST = 16
MXU_DTYPE = jnp.bfloat16
VMEM_LIMIT_BYTES = 52 * 1024 * 1024


NEG_LOG2_E = -1.4426950408889634


def _sigmoid(x):
    return 1.0 / (1.0 + jnp.exp2(x * NEG_LOG2_E))


def _silu(x):
    return x * _sigmoid(x)


def _log_sigmoid(x):
    return jnp.minimum(x, 0.0) - jnp.log1p(jnp.exp(-jnp.abs(x)))


def _dot(a, b):
    return jnp.dot(a.astype(MXU_DTYPE), b.astype(MXU_DTYPE), preferred_element_type=jnp.float32)


def _dot_nt(a, b):
    return lax.dot_general(a.astype(MXU_DTYPE), b.astype(MXU_DTYPE),
                           (((1,), (1,)), ((), ())), preferred_element_type=jnp.float32)


def _dot_f32_lhs(a, b):
    r = a.shape[0]
    hi = a.astype(jnp.bfloat16).astype(jnp.float32)
    rest = a - hi
    mid = rest.astype(jnp.bfloat16).astype(jnp.float32)
    parts = jnp.concatenate([hi, mid, rest - mid], axis=0).astype(jnp.bfloat16)
    out = jnp.dot(parts, b.astype(jnp.bfloat16), preferred_element_type=jnp.float32)
    return out[:r] + out[r:2 * r] + out[2 * r:]


def _lane_prefix_max(x):
    lane = lax.broadcasted_iota(jnp.int32, x.shape, 1)
    shift = 1
    while shift < x.shape[1]:
        x = jnp.maximum(x, jnp.where(lane >= shift, pltpu.roll(x, shift, 1), -jnp.inf))
        shift *= 2
    return x


def _column_of(row):
    n = row.shape[1]
    return jnp.broadcast_to(row, (n, n)).T


def _last_lane(x):
    return jnp.broadcast_to(x[:, x.shape[1] - 1:], x.shape)


def _row_of(col):
    n = col.shape[0]
    return jnp.broadcast_to(col, (n, n)).T[0:1]


def _post_norm(x, branch, g, b, alpha):
    y = alpha * x + branch
    mu = jnp.mean(y, axis=-1, keepdims=True)
    cen = y - mu
    var = jnp.mean(cen * cen, axis=-1, keepdims=True)
    return cen * lax.rsqrt(var + LN_EPS) * g + b


def _mlstm_layer_kernel(x_ref, w_in_ref, wg_ref, bg_ref, cw_ref, cb_ref, wq_ref, wkt_ref, wv_ref,
                        ng_ref, sk_ref, wo_ref, lg_ref, lb_ref, o_ref,
                        xb_s, g_s, acc_s, c_s, mx_s, en_s, ws_s, wc_s, mp_s,
                        m_st, cst_s, nst_s, hist_s, *head_bufs, alpha, q_scale):
    n_chunks = TILE // CHUNK
    dv = g_s.shape[2]
    bufs = (head_bufs[:len(head_bufs) // 2], head_bufs[len(head_bufs) // 2:])
    first_tile = pl.program_id(1) == 0

    @pl.when(first_tile)
    def _():
        m_st[...] = jnp.zeros_like(m_st)
        cst_s[...] = jnp.zeros_like(cst_s)
        nst_s[...] = jnp.zeros_like(nst_s)
        hist_s[...] = jnp.zeros_like(hist_s)

    xb_s[...] = x_ref[...].astype(xb_s.dtype)

    gates = _dot_nt(wg_ref[...], xb_s[...]) + bg_ref[:, 0:1]
    log_i = gates[:N_HEADS]
    log_f = _log_sigmoid(gates[N_HEADS:])
    row_id = lax.broadcasted_iota(jnp.int32, (CHUNK, CHUNK), 0)
    col_id = lax.broadcasted_iota(jnp.int32, (CHUNK, CHUNK), 1)
    tri = row_id >= col_id
    lf_rows = jnp.concatenate([log_f[:, c * CHUNK:(c + 1) * CHUNK] for c in range(n_chunks)], axis=0)
    bcum_rows = _dot_f32_lhs(lf_rows, jnp.where(row_id <= col_id, 1.0, 0.0))
    li_rows = jnp.concatenate([log_i[:, c * CHUNK:(c + 1) * CHUNK] for c in range(n_chunks)], axis=0)
    cc_rows = li_rows - bcum_rows
    cmax_rows = _lane_prefix_max(cc_rows)
    m_prev = m_st[...]
    for c in range(n_chunks):
        blk = slice(c * N_HEADS, (c + 1) * N_HEADS)
        bcum = bcum_rows[blk]
        cc = cc_rows[blk]
        mx = jnp.maximum(m_prev, cmax_rows[blk])
        mx_last = _last_lane(mx)
        rows = (cc, mx, jnp.exp(-(bcum + mx)), jnp.exp(cc - mx_last), jnp.exp(m_prev - mx_last), m_prev)
        for dst, val in zip((c_s, mx_s, en_s, ws_s, wc_s, mp_s), rows):
            for hh in range(N_HEADS):
                dst[c, hh] = val[hh:hh + 1]
        m_prev = _last_lane(bcum) + mx_last
    m_st[...] = m_prev

    def project(h, slot, part):
        dst = bufs[slot][part]
        start = part * N_HEADS * dv + h * dv
        out = _dot(xb_s[...], w_in_ref[:, start:start + dv])
        if part == 0:
            dst[0:CONV_HIST, :] = hist_s[h]
            dst[CONV_HIST:, :] = out
            hist_s[h] = dst[TILE:TILE + CONV_HIST, :]
        else:
            dst[...] = out

    def mix(h, slot, fillers):
        xm_s, z_s, o_s, xc_s, q_s, kt_s, v_s = bufs[slot]
        cw = cw_ref[h]
        xe = xm_s[...]
        xe1 = pltpu.roll(xe, 1, 0)
        pair = cw[1:2] * xe + cw[0:1] * xe1
        conv = cb_ref[h] + cw[3:4] * xe + cw[2:3] * xe1 + pltpu.roll(pair, 2, 0)
        xc = _silu(conv[CONV_HIST:])
        xc_s[...] = xc
        xc_b = xc.astype(MXU_DTYPE)
        q_s[...] = _dot(xc_b, wq_ref[h]) * q_scale
        kt_s[...] = _dot_nt(wkt_ref[h], xc_b)
        v_s[...] = _dot(xm_s[CONV_HIST:, :], wv_ref[h]).astype(v_s.dtype)

        for c in range(n_chunks):
            rows = slice(c * CHUNK, (c + 1) * CHUNK)
            m_in = mp_s[c, h][:, 0:1]
            mx_col = _column_of(mx_s[c, h])
            decay = jnp.where(tri, jnp.exp(c_s[c, h] - mx_col), 0.0)
            qc = q_s[rows, :]
            ktc = kt_s[:, rows]
            vc = v_s[rows, :]
            scores = _dot(qc, ktc)
            kts = ktc * ws_s[c, h]
            kv = _dot(kts, vc)
            if fillers[c] is not None:
                fillers[c]()
            s = scores * decay
            qs = qc * jnp.exp(m_in - mx_col)
            c_prev = cst_s[h]
            n_prev = nst_s[h]
            den = jnp.sum(s + qs * n_prev, axis=-1, keepdims=True)
            num = _dot(jnp.concatenate([s, qs], axis=1),
                       jnp.concatenate([vc, c_prev.astype(MXU_DTYPE)], axis=0))
            hv = num / jnp.maximum(jnp.abs(den), _column_of(en_s[c, h])[:, 0:1])
            wc = wc_s[c, h][:, 0:1]
            cst_s[h] = wc * c_prev + kv
            nst_s[h] = wc * n_prev + _row_of(jnp.sum(kts, axis=-1, keepdims=True))
            mu = jnp.mean(hv, axis=-1, keepdims=True)
            cen = hv - mu
            var = jnp.mean(cen * cen, axis=-1, keepdims=True)
            hn = cen * lax.rsqrt(var + LN_EPS) * ng_ref[h]
            hb = _sigmoid(o_s[rows, :]) * hn + sk_ref[h] * xc_s[rows, :]
            g_s[h, rows, :] = (hb * _silu(z_s[rows, :])).astype(g_s.dtype)

    half = acc_s.shape[1] // 2

    def out_project(h, part):
        cols = slice(part * half, (part + 1) * half)
        contrib = _dot(g_s[h], wo_ref[h, :, cols])
        if h == 0:
            acc_s[:, cols] = contrib
        else:
            acc_s[:, cols] += contrib

    for part in range(3):
        project(0, 0, part)
    for h in range(N_HEADS):
        slot = h % 2
        fillers = [None] * n_chunks
        if h + 1 < N_HEADS:
            project(h + 1, 1 - slot, 0)
            fillers[0] = functools.partial(project, h + 1, 1 - slot, 1)
            fillers[1] = functools.partial(project, h + 1, 1 - slot, 2)
        if h > 0:
            fillers[2] = functools.partial(out_project, h - 1, 0)
            fillers[3] = functools.partial(out_project, h - 1, 1)
        mix(h, slot, fillers)
    branch = acc_s[...] + _dot(g_s[N_HEADS - 1], wo_ref[N_HEADS - 1])
    o_ref[...] = _post_norm(x_ref[...], branch, lg_ref[...], lb_ref[...], alpha)


def _resident(shape):
    zeros = (0,) * len(shape)
    return pl.BlockSpec(shape, lambda b, i: zeros, pipeline_mode=pl.Buffered(1))


def _mlstm_layer(x, w_in, conv_w, conv_b, w_q, w_k, w_v, b_i, b_f, norm_g, skip, w_out,
                 ln_g, ln_b, alpha):
    B, S, D = x.shape
    H = N_HEADS
    dv = w_v.shape[-1]
    dk = w_q.shape[-1]
    E = H * dv
    assert S % TILE == 0 and TILE % CHUNK == 0 and CHUNK == dk and H % 2 == 0
    assert conv_w.shape[0] == CONV_K == 4 and CONV_HIST >= CONV_K - 1
    wdt = MXU_DTYPE
    w_main = w_in[:, :3 * E].astype(wdt)
    w_gate = w_in[:, 3 * E:].T.astype(wdt)
    b_gate = jnp.broadcast_to(jnp.concatenate([b_i, b_f])[:, None], (2 * H, 128)).astype(jnp.float32)
    args = (
        x, w_main, w_gate, b_gate,
        conv_w.reshape(CONV_K, H, dv).transpose(1, 0, 2), conv_b.reshape(H, 1, dv),
        w_q.astype(wdt), w_k.transpose(0, 2, 1).astype(wdt), w_v.astype(wdt),
        norm_g.reshape(H, 1, dv), skip.reshape(H, 1, dv),
        w_out.reshape(H, dv, D).astype(wdt), ln_g.reshape(1, D), ln_b.reshape(1, D),
    )
    x_spec = pl.BlockSpec((None, TILE, D), lambda b, i: (b, i, 0))
    in_specs = [x_spec] + [_resident(a.shape) for a in args[1:]]
    n_chunks = TILE // CHUNK
    row_scratch = pltpu.VMEM((n_chunks, H, 1, CHUNK), jnp.float32)
    head_bufs = [
        pltpu.VMEM((TILE + CONV_HIST, dv), jnp.float32),
        pltpu.VMEM((TILE, dv), jnp.float32),
        pltpu.VMEM((TILE, dv), jnp.float32),
        pltpu.VMEM((TILE, dv), jnp.float32),
        pltpu.VMEM((TILE, dk), jnp.float32),
        pltpu.VMEM((dk, TILE), jnp.float32),
        pltpu.VMEM((TILE, dv), wdt),
    ]
    scratch = [
        pltpu.VMEM((TILE, D), wdt),
        pltpu.VMEM((H, TILE, dv), wdt),
        pltpu.VMEM((TILE, D), jnp.float32),
        row_scratch, row_scratch, row_scratch, row_scratch, row_scratch, row_scratch,
        pltpu.VMEM((H, CHUNK), jnp.float32),
        pltpu.VMEM((H, dk, dv), jnp.float32),
        pltpu.VMEM((H, 1, dk), jnp.float32),
        pltpu.VMEM((H, CONV_HIST, dv), jnp.float32),
    ] + head_bufs + head_bufs
    kern = functools.partial(_mlstm_layer_kernel, alpha=alpha, q_scale=float(dk) ** -0.5)
    return pl.pallas_call(
        kern,
        grid=(B, S // TILE),
        in_specs=in_specs,
        out_specs=pl.BlockSpec((None, TILE, D), lambda b, i: (b, i, 0)),
        out_shape=jax.ShapeDtypeStruct((B, S, D), x.dtype),
        scratch_shapes=scratch,
        compiler_params=pltpu.CompilerParams(
            dimension_semantics=("arbitrary", "arbitrary"), vmem_limit_bytes=VMEM_LIMIT_BYTES),
        name="mlstm_layer",
    )(*args)


def _pool_layer_kernel(x_ref, w_in_ref, wg_ref, sc_ref, wo_ref, lg_ref, lb_ref, o_ref,
                       xb_s, u_s, gt_s, hist_s, *, alpha):
    n_groups, _, cg = u_s.shape
    first_tile = pl.program_id(1) == 0

    @pl.when(first_tile)
    def _():
        hist_s[...] = jnp.zeros_like(hist_s)

    xb_s[...] = x_ref[...].astype(xb_s.dtype)
    head_row = lax.broadcasted_iota(jnp.int32, (POOL_HIST, cg), 0)

    for g, win in enumerate(POOL_WINDOWS):
        u = _dot(xb_s[...], w_in_ref[:, g * cg:(g + 1) * cg])
        z = _dot(xb_s[...], w_in_ref[:, n_groups * cg + g * cg:n_groups * cg + (g + 1) * cg])
        u_s[g, 0:POOL_HIST, :] = hist_s[g]
        u_s[g, POOL_HIST:, :] = u
        hist_s[g] = u_s[g, POOL_TILE:POOL_TILE + POOL_HIST, :]
        wsum = u_s[g]
        span = 1
        while span < win:
            wsum = wsum + pltpu.roll(wsum, span, 0)
            span *= 2
        wsum = wsum[POOL_HIST:]
        cnt = jnp.where(first_tile, jnp.minimum(head_row + 1, win), win).astype(jnp.float32)
        pooled = jnp.concatenate(
            [wsum[:POOL_HIST] / cnt - u[:POOL_HIST], wsum[POOL_HIST:] * (1.0 / win) - u[POOL_HIST:]],
            axis=0)
        mixed = _dot(pooled, wg_ref[g]) * sc_ref[g]
        gt_s[:, g * cg:(g + 1) * cg] = (mixed * _silu(z)).astype(gt_s.dtype)

    branch = _dot(gt_s[...], wo_ref[...])
    o_ref[...] = _post_norm(x_ref[...], branch, lg_ref[...], lb_ref[...], alpha)


def _pool_layer(x, w_in, w_group, scale, w_out, ln_g, ln_b, alpha):
    B, S, D = x.shape
    G, cg, _ = w_group.shape
    E = G * cg
    assert S % POOL_TILE == 0 and G == len(POOL_WINDOWS) and max(POOL_WINDOWS) <= POOL_HIST
    wdt = MXU_DTYPE
    args = (x, w_in.astype(wdt), w_group.astype(wdt), scale.reshape(G, 1, cg),
            w_out.astype(wdt), ln_g.reshape(1, D), ln_b.reshape(1, D))
    in_specs = ([pl.BlockSpec((None, POOL_TILE, D), lambda b, i: (b, i, 0))]
                + [_resident(a.shape) for a in args[1:]])
    scratch = [
        pltpu.VMEM((POOL_TILE, D), wdt),
        pltpu.VMEM((G, POOL_TILE + POOL_HIST, cg), jnp.float32),
        pltpu.VMEM((POOL_TILE, E), wdt),
        pltpu.VMEM((G, POOL_HIST, cg), jnp.float32),
    ]
    return pl.pallas_call(
        functools.partial(_pool_layer_kernel, alpha=alpha),
        grid=(B, S // POOL_TILE),
        in_specs=in_specs,
        out_specs=pl.BlockSpec((None, POOL_TILE, D), lambda b, i: (b, i, 0)),
        out_shape=jax.ShapeDtypeStruct((B, S, D), x.dtype),
        scratch_shapes=scratch,
        compiler_params=pltpu.CompilerParams(
            dimension_semantics=("arbitrary", "arbitrary"), vmem_limit_bytes=VMEM_LIMIT_BYTES),
        name="pool_layer",
    )(*args)


def kernel(x, mlstm_w_in, mlstm_conv_w, mlstm_conv_b, mlstm_w_q, mlstm_w_k, mlstm_w_v, mlstm_b_i,
           mlstm_b_f, mlstm_norm_g, mlstm_skip, mlstm_w_out, pool_w_in, pool_w_group, pool_scale,
           pool_w_out, post_ln_g, post_ln_b):
    depth = post_ln_g.shape[0]
    alpha = (2.0 * depth) ** 0.25
    for i in range(depth):
        j = i // 2
        if i % 2 == 0:
            x = _mlstm_layer(x, mlstm_w_in[j], mlstm_conv_w[j], mlstm_conv_b[j], mlstm_w_q[j],
                             mlstm_w_k[j], mlstm_w_v[j], mlstm_b_i[j], mlstm_b_f[j],
                             mlstm_norm_g[j], mlstm_skip[j], mlstm_w_out[j],
                             post_ln_g[i], post_ln_b[i], alpha)
        else:
            x = _pool_layer(x, pool_w_in[j], pool_w_group[j], pool_scale[j], pool_w_out[j],
                            post_ln_g[i], post_ln_b[i], alpha)
    return x
```

```python
import functools

import jax
import jax.numpy as jnp
from jax import lax
from jax.experimental import pallas as pl
from jax.experimental.pallas import tpu as pltpu

N_HEADS = 8
CONV_K = 4
CHUNK = 128
POOL_WINDOWS = (2, 4, 8, 16)
LN_EPS = 1e-5
TILE = 512
POOL_TILE = 1024
CONV_HIST = 8
POOL_HIST = 16
MXU_DTYPE = jnp.bfloat16
VMEM_LIMIT_BYTES = 52 * 1024 * 1024


NEG_LOG2_E = -1.4426950408889634


def _sigmoid(x):
    return 1.0 / (1.0 + jnp.exp2(x * NEG_LOG2_E))


def _silu(x):
    return x * _sigmoid(x)


def _log_sigmoid(x):
    return jnp.minimum(x, 0.0) - jnp.log1p(jnp.exp(-jnp.abs(x)))


def _dot(a, b):
    return jnp.dot(a.astype(MXU_DTYPE), b.astype(MXU_DTYPE), preferred_element_type=jnp.float32)


def _dot_nt(a, b):
    return lax.dot_general(a.astype(MXU_DTYPE), b.astype(MXU_DTYPE),
                           (((1,), (1,)), ((), ())), preferred_element_type=jnp.float32)


def _dot_f32_lhs(a, b):
    r = a.shape[0]
    hi = a.astype(jnp.bfloat16).astype(jnp.float32)
    rest = a - hi
    mid = rest.astype(jnp.bfloat16).astype(jnp.float32)
    parts = jnp.concatenate([hi, mid, rest - mid], axis=0).astype(jnp.bfloat16)
    out = jnp.dot(parts, b.astype(jnp.bfloat16), preferred_element_type=jnp.float32)
    return out[:r] + out[r:2 * r] + out[2 * r:]


def _lane_prefix_max(x):
    lane = lax.broadcasted_iota(jnp.int32, x.shape, 1)
    shift = 1
    while shift < x.shape[1]:
        x = jnp.maximum(x, jnp.where(lane >= shift, pltpu.roll(x, shift, 1), -jnp.inf))
        shift *= 2
    return x


def _column_of(row):
    n = row.shape[1]
    return jnp.broadcast_to(row, (n, n)).T


def _last_lane(x):
    return jnp.broadcast_to(x[:, x.shape[1] - 1:], x.shape)


def _row_of(col):
    n = col.shape[0]
    return jnp.broadcast_to(col, (n, n)).T[0:1]


def _post_norm(x, branch, g, b, alpha):
    y = alpha * x + branch
    mu = jnp.mean(y, axis=-1, keepdims=True)
    cen = y - mu
    var = jnp.mean(cen * cen, axis=-1, keepdims=True)
    return cen * lax.rsqrt(var + LN_EPS) * g + b


def _mlstm_layer_kernel(x_ref, w_in_ref, wg_ref, bg_ref, cw_ref, cb_ref, wq_ref, wkt_ref, wv_ref,
                        ng_ref, sk_ref, wo_ref, lg_ref, lb_ref, o_ref,
                        xb_s, g_s, acc_s, c_s, mx_s, en_s, ws_s, wc_s, mp_s,
                        m_st, cst_s, nst_s, hist_s, *head_bufs, alpha, q_scale):
    n_chunks = TILE // CHUNK
    dv = g_s.shape[2]
    bufs = (head_bufs[:len(head_bufs) // 2], head_bufs[len(head_bufs) // 2:])
    first_tile = pl.program_id(1) == 0

    @pl.when(first_tile)
    def _():
        m_st[...] = jnp.zeros_like(m_st)
        cst_s[...] = jnp.zeros_like(cst_s)
        nst_s[...] = jnp.zeros_like(nst_s)
        hist_s[...] = jnp.zeros_like(hist_s)

    xb_s[...] = x_ref[...].astype(xb_s.dtype)

    gates = _dot_nt(wg_ref[...], xb_s[...]) + bg_ref[:, 0:1]
    log_i = gates[:N_HEADS]
    log_f = _log_sigmoid(gates[N_HEADS:])
    row_id = lax.broadcasted_iota(jnp.int32, (CHUNK, CHUNK), 0)
    col_id = lax.broadcasted_iota(jnp.int32, (CHUNK, CHUNK), 1)
    tri = row_id >= col_id
    lf_rows = jnp.concatenate([log_f[:, c * CHUNK:(c + 1) * CHUNK] for c in range(n_chunks)], axis=0)
    bcum_rows = _dot_f32_lhs(lf_rows, jnp.where(row_id <= col_id, 1.0, 0.0))
    li_rows = jnp.concatenate([log_i[:, c * CHUNK:(c + 1) * CHUNK] for c in range(n_chunks)], axis=0)
    cc_rows = li_rows - bcum_rows
    cmax_rows = _lane_prefix_max(cc_rows)
    m_prev = m_st[...]
    for c in range(n_chunks):
        blk = slice(c * N_HEADS, (c + 1) * N_HEADS)
        bcum = bcum_rows[blk]
        cc = cc_rows[blk]
        mx = jnp.maximum(m_prev, cmax_rows[blk])
        mx_last = _last_lane(mx)
        rows = (cc, mx, jnp.exp(-(bcum + mx)), jnp.exp(cc - mx_last), jnp.exp(m_prev - mx_last), m_prev)
        for dst, val in zip((c_s, mx_s, en_s, ws_s, wc_s, mp_s), rows):
            for hh in range(N_HEADS):
                dst[c, hh] = val[hh:hh + 1]
        m_prev = _last_lane(bcum) + mx_last
    m_st[...] = m_prev

    def project(h, slot, part):
        dst = bufs[slot][part]
        start = part * N_HEADS * dv + h * dv
        out = _dot(xb_s[...], w_in_ref[:, start:start + dv])
        if part == 0:
            dst[0:CONV_HIST, :] = hist_s[h]
            dst[CONV_HIST:, :] = out
            hist_s[h] = dst[TILE:TILE + CONV_HIST, :]
        else:
            dst[...] = out

    def mix(h, slot, fillers):
        xm_s, z_s, o_s, xc_s, q_s, kt_s, v_s = bufs[slot]
        cw = cw_ref[h]
        xe = xm_s[...]
        xe1 = pltpu.roll(xe, 1, 0)
        pair = cw[1:2] * xe + cw[0:1] * xe1
        conv = cb_ref[h] + cw[3:4] * xe + cw[2:3] * xe1 + pltpu.roll(pair, 2, 0)
        xc = _silu(conv[CONV_HIST:])
        xc_s[...] = xc
        xc_b = xc.astype(MXU_DTYPE)
        q_s[...] = _dot(xc_b, wq_ref[h]) * q_scale
        kt_s[...] = _dot_nt(wkt_ref[h], xc_b)
        v_s[...] = _dot(xm_s[CONV_HIST:, :], wv_ref[h]).astype(v_s.dtype)

        for c in range(n_chunks):
            rows = slice(c * CHUNK, (c + 1) * CHUNK)
            m_in = mp_s[c, h][:, 0:1]
            mx_col = _column_of(mx_s[c, h])
            decay = jnp.where(tri, jnp.exp(c_s[c, h] - mx_col), 0.0)
            qc = q_s[rows, :]
            ktc = kt_s[:, rows]
            vc = v_s[rows, :]
            scores = _dot(qc, ktc)
            kts = ktc * ws_s[c, h]
            kv = _dot(kts, vc)
            if fillers[c] is not None:
                fillers[c]()
            s = scores * decay
            qs = qc * jnp.exp(m_in - mx_col)
            c_prev = cst_s[h]
            n_prev = nst_s[h]
            den = jnp.sum(s + qs * n_prev, axis=-1, keepdims=True)
            num = _dot(jnp.concatenate([s, qs], axis=1),
                       jnp.concatenate([vc, c_prev.astype(MXU_DTYPE)], axis=0))
            hv = num / jnp.maximum(jnp.abs(den), _column_of(en_s[c, h])[:, 0:1])
            wc = wc_s[c, h][:, 0:1]
            cst_s[h] = wc * c_prev + kv
            nst_s[h] = wc * n_prev + _row_of(jnp.sum(kts, axis=-1, keepdims=True))
            mu = jnp.mean(hv, axis=-1, keepdims=True)
            cen = hv - mu
            var = jnp.mean(cen * cen, axis=-1, keepdims=True)
            hn = cen * lax.rsqrt(var + LN_EPS) * ng_ref[h]
            hb = _sigmoid(o_s[rows, :]) * hn + sk_ref[h] * xc_s[rows, :]
            g_s[h, rows, :] = (hb * _silu(z_s[rows, :])).astype(g_s.dtype)

    half = acc_s.shape[1] // 2

    def out_project(h, part):
        cols = slice(part * half, (part + 1) * half)
        contrib = _dot(g_s[h], wo_ref[h, :, cols])
        if h == 0:
            acc_s[:, cols] = contrib
        else:
            acc_s[:, cols] += contrib

    for part in range(3):
        project(0, 0, part)
    for h in range(N_HEADS):
        slot = h % 2
        fillers = [None] * n_chunks
        if h + 1 < N_HEADS:
            project(h + 1, 1 - slot, 0)
            fillers[0] = functools.partial(project, h + 1, 1 - slot, 1)
            fillers[1] = functools.partial(project, h + 1, 1 - slot, 2)
        if h > 0:
            fillers[2] = functools.partial(out_project, h - 1, 0)
            fillers[3] = functools.partial(out_project, h - 1, 1)
        mix(h, slot, fillers)
    branch = acc_s[...] + _dot(g_s[N_HEADS - 1], wo_ref[N_HEADS - 1])
    o_ref[...] = _post_norm(x_ref[...], branch, lg_ref[...], lb_ref[...], alpha)


def _resident(shape):
    zeros = (0,) * len(shape)
    return pl.BlockSpec(shape, lambda b, i: zeros, pipeline_mode=pl.Buffered(1))


def _mlstm_layer(x, w_in, conv_w, conv_b, w_q, w_k, w_v, b_i, b_f, norm_g, skip, w_out,
                 ln_g, ln_b, alpha):
    B, S, D = x.shape
    H = N_HEADS
    dv = w_v.shape[-1]
    dk = w_q.shape[-1]
    E = H * dv
    assert S % TILE == 0 and TILE % CHUNK == 0 and CHUNK == dk and H % 2 == 0
    assert conv_w.shape[0] == CONV_K == 4 and CONV_HIST >= CONV_K - 1
    wdt = MXU_DTYPE
    w_main = w_in.astype(wdt)
    w_gate = w_in[:, 3 * E:].T.astype(wdt)
    b_gate = jnp.broadcast_to(jnp.concatenate([b_i, b_f])[:, None], (2 * H, 128)).astype(jnp.float32)
    args = (
        x, w_main, w_gate, b_gate,
        conv_w.reshape(CONV_K, H, dv).transpose(1, 0, 2), conv_b.reshape(H, 1, dv),
        w_q.astype(wdt), w_k.transpose(0, 2, 1).astype(wdt), w_v.astype(wdt),
        norm_g.reshape(H, 1, dv), skip.reshape(H, 1, dv),
        w_out.reshape(H, dv, D).astype(wdt), ln_g.reshape(1, D), ln_b.reshape(1, D),
    )
    x_spec = pl.BlockSpec((None, TILE, D), lambda b, i: (b, i, 0))
    in_specs = [x_spec, _resident((D, 3 * E))] + [_resident(a.shape) for a in args[2:]]
    n_chunks = TILE // CHUNK
    row_scratch = pltpu.VMEM((n_chunks, H, 1, CHUNK), jnp.float32)
    head_bufs = [
        pltpu.VMEM((TILE + CONV_HIST, dv), jnp.float32),
        pltpu.VMEM((TILE, dv), jnp.float32),
        pltpu.VMEM((TILE, dv), jnp.float32),
        pltpu.VMEM((TILE, dv), jnp.float32),
        pltpu.VMEM((TILE, dk), jnp.float32),
        pltpu.VMEM((dk, TILE), jnp.float32),
        pltpu.VMEM((TILE, dv), wdt),
    ]
    scratch = [
        pltpu.VMEM((TILE, D), wdt),
        pltpu.VMEM((H, TILE, dv), wdt),
        pltpu.VMEM((TILE, D), jnp.float32),
        row_scratch, row_scratch, row_scratch, row_scratch, row_scratch, row_scratch,
        pltpu.VMEM((H, CHUNK), jnp.float32),
        pltpu.VMEM((H, dk, dv), jnp.float32),
        pltpu.VMEM((H, 1, dk), jnp.float32),
        pltpu.VMEM((H, CONV_HIST, dv), jnp.float32),
    ] + head_bufs + head_bufs
    kern = functools.partial(_mlstm_layer_kernel, alpha=alpha, q_scale=float(dk) ** -0.5)
    return pl.pallas_call(
        kern,
        grid=(B, S // TILE),
        in_specs=in_specs,
        out_specs=pl.BlockSpec((None, TILE, D), lambda b, i: (b, i, 0)),
        out_shape=jax.ShapeDtypeStruct((B, S, D), x.dtype),
        scratch_shapes=scratch,
        compiler_params=pltpu.CompilerParams(
            dimension_semantics=("arbitrary", "arbitrary"), vmem_limit_bytes=VMEM_LIMIT_BYTES),
        name="mlstm_layer",
    )(*args)


def _pool_layer_kernel(x_ref, w_in_ref, wg_ref, sc_ref, wo_ref, lg_ref, lb_ref, o_ref,
                       xb_s, u_s, gt_s, hist_s, *, alpha):
    n_groups, _, cg = u_s.shape
    first_tile = pl.program_id(1) == 0

    @pl.when(first_tile)
    def _():
        hist_s[...] = jnp.zeros_like(hist_s)

    xb_s[...] = x_ref[...].astype(xb_s.dtype)
    head_row = lax.broadcasted_iota(jnp.int32, (POOL_HIST, cg), 0)

    for g, win in enumerate(POOL_WINDOWS):
        u = _dot(xb_s[...], w_in_ref[:, g * cg:(g + 1) * cg])
        z = _dot(xb_s[...], w_in_ref[:, n_groups * cg + g * cg:n_groups * cg + (g + 1) * cg])
        u_s[g, 0:POOL_HIST, :] = hist_s[g]
        u_s[g, POOL_HIST:, :] = u
        hist_s[g] = u_s[g, POOL_TILE:POOL_TILE + POOL_HIST, :]
        wsum = u_s[g]
        span = 1
        while span < win:
            wsum = wsum + pltpu.roll(wsum, span, 0)
            span *= 2
        wsum = wsum[POOL_HIST:]
        cnt = jnp.where(first_tile, jnp.minimum(head_row + 1, win), win).astype(jnp.float32)
        pooled = jnp.concatenate(
            [wsum[:POOL_HIST] / cnt - u[:POOL_HIST], wsum[POOL_HIST:] * (1.0 / win) - u[POOL_HIST:]],
            axis=0)
        mixed = _dot(pooled, wg_ref[g]) * sc_ref[g]
        gt_s[:, g * cg:(g + 1) * cg] = (mixed * _silu(z)).astype(gt_s.dtype)

    branch = _dot(gt_s[...], wo_ref[...])
    o_ref[...] = _post_norm(x_ref[...], branch, lg_ref[...], lb_ref[...], alpha)


def _pool_layer(x, w_in, w_group, scale, w_out, ln_g, ln_b, alpha):
    B, S, D = x.shape
    G, cg, _ = w_group.shape
    E = G * cg
    assert S % POOL_TILE == 0 and G == len(POOL_WINDOWS) and max(POOL_WINDOWS) <= POOL_HIST
    wdt = MXU_DTYPE
    args = (x, w_in.astype(wdt), w_group.astype(wdt), scale.reshape(G, 1, cg),
            w_out.astype(wdt), ln_g.reshape(1, D), ln_b.reshape(1, D))
    in_specs = ([pl.BlockSpec((None, POOL_TILE, D), lambda b, i: (b, i, 0))]
                + [_resident(a.shape) for a in args[1:]])
    scratch = [
        pltpu.VMEM((POOL_TILE, D), wdt),
        pltpu.VMEM((G, POOL_TILE + POOL_HIST, cg), jnp.float32),
        pltpu.VMEM((POOL_TILE, E), wdt),
        pltpu.VMEM((G, POOL_HIST, cg), jnp.float32),
    ]
    return pl.pallas_call(
        functools.partial(_pool_layer_kernel, alpha=alpha),
        grid=(B, S // POOL_TILE),
        in_specs=in_specs,
        out_specs=pl.BlockSpec((None, POOL_TILE, D), lambda b, i: (b, i, 0)),
        out_shape=jax.ShapeDtypeStruct((B, S, D), x.dtype),
        scratch_shapes=scratch,
        compiler_params=pltpu.CompilerParams(
            dimension_semantics=("arbitrary", "arbitrary"), vmem_limit_bytes=VMEM_LIMIT_BYTES),
        name="pool_layer",
    )(*args)


def kernel(x, mlstm_w_in, mlstm_conv_w, mlstm_conv_b, mlstm_w_q, mlstm_w_k, mlstm_w_v, mlstm_b_i,
           mlstm_b_f, mlstm_norm_g, mlstm_skip, mlstm_w_out, pool_w_in, pool_w_group, pool_scale,
           pool_w_out, post_ln_g, post_ln_b):
    depth = post_ln_g.shape[0]
    alpha = (2.0 * depth) ** 0.25
    for i in range(depth):
        j = i // 2
        if i % 2 == 0:
            x = _mlstm_layer(x, mlstm_w_in[j], mlstm_conv_w[j], mlstm_conv_b[j], mlstm_w_q[j],
                             mlstm_w_k[j], mlstm_w_v[j], mlstm_b_i[j], mlstm_b_f[j],
                             mlstm_norm_g[j], mlstm_skip[j], mlstm_w_out[j],
                             post_ln_g[i], post_ln_b[i], alpha)
        else:
            x = _pool_layer(x, pool_w_in[j], pool_w_group[j], pool_scale[j], pool_w_out[j],
                            post_ln_g[i], post_ln_b[i], alpha)
    return x
```

```python
import functools

import jax
import jax.numpy as jnp
from jax import lax
from jax.experimental import pallas as pl
from jax.experimental.pallas import tpu as pltpu

N_HEADS = 8
CONV_K = 4
CHUNK = 128
POOL_WINDOWS = (2, 4, 8, 16)
LN_EPS = 1e-5
TILE = 512
POOL_TILE = 1024
POOL_OUT_BLOCKS = 4
CONV_HIST = 8
POOL_HIST = 16
MXU_DTYPE = jnp.bfloat16
VMEM_LIMIT_BYTES = 52 * 1024 * 1024


NEG_LOG2_E = -1.4426950408889634


def _sigmoid(x):
    return 1.0 / (1.0 + jnp.exp2(x * NEG_LOG2_E))


def _silu(x):
    return x * _sigmoid(x)


def _log_sigmoid(x):
    return jnp.minimum(x, 0.0) - jnp.log1p(jnp.exp(-jnp.abs(x)))


def _dot(a, b):
    return jnp.dot(a.astype(MXU_DTYPE), b.astype(MXU_DTYPE), preferred_element_type=jnp.float32)


def _dot_nt(a, b):
    return lax.dot_general(a.astype(MXU_DTYPE), b.astype(MXU_DTYPE),
                           (((1,), (1,)), ((), ())), preferred_element_type=jnp.float32)


def _dot_f32_lhs(a, b):
    r = a.shape[0]
    hi = a.astype(jnp.bfloat16).astype(jnp.float32)
    rest = a - hi
    mid = rest.astype(jnp.bfloat16).astype(jnp.float32)
    parts = jnp.concatenate([hi, mid, rest - mid], axis=0).astype(jnp.bfloat16)
    out = jnp.dot(parts, b.astype(jnp.bfloat16), preferred_element_type=jnp.float32)
    return out[:r] + out[r:2 * r] + out[2 * r:]


def _lane_prefix_max(x):
    lane = lax.broadcasted_iota(jnp.int32, x.shape, 1)
    shift = 1
    while shift < x.shape[1]:
        x = jnp.maximum(x, jnp.where(lane >= shift, pltpu.roll(x, shift, 1), -jnp.inf))
        shift *= 2
    return x


def _column_of(row):
    n = row.shape[1]
    return jnp.broadcast_to(row, (n, n)).T


def _last_lane(x):
    return jnp.broadcast_to(x[:, x.shape[1] - 1:], x.shape)


def _row_of(col):
    n = col.shape[0]
    return jnp.broadcast_to(col, (n, n)).T[0:1]


def _post_norm(x, branch, g, b, alpha):
    y = alpha * x + branch
    mu = jnp.mean(y, axis=-1, keepdims=True)
    cen = y - mu
    var = jnp.mean(cen * cen, axis=-1, keepdims=True)
    return cen * lax.rsqrt(var + LN_EPS) * g + b


def _mlstm_layer_kernel(x_ref, w_in_ref, wg_ref, bg_ref, cw_ref, cb_ref, wq_ref, wkt_ref, wv_ref,
                        ng_ref, sk_ref, wo_ref, lg_ref, lb_ref, o_ref,
                        xb_s, g_s, acc_s, c_s, mx_s, en_s, ws_s, wc_s, mp_s,
                        m_st, cst_s, nst_s, hist_s, *head_bufs, alpha, q_scale):
    n_chunks = TILE // CHUNK
    dv = g_s.shape[2]
    bufs = (head_bufs[:len(head_bufs) // 2], head_bufs[len(head_bufs) // 2:])
    first_tile = pl.program_id(1) == 0

    @pl.when(first_tile)
    def _():
        m_st[...] = jnp.zeros_like(m_st)
        cst_s[...] = jnp.zeros_like(cst_s)
        nst_s[...] = jnp.zeros_like(nst_s)
        hist_s[...] = jnp.zeros_like(hist_s)

    xb_s[...] = x_ref[...].astype(xb_s.dtype)

    gates = _dot_nt(wg_ref[...], xb_s[...]) + bg_ref[:, 0:1]
    log_i = gates[:N_HEADS]
    log_f = _log_sigmoid(gates[N_HEADS:])
    row_id = lax.broadcasted_iota(jnp.int32, (CHUNK, CHUNK), 0)
    col_id = lax.broadcasted_iota(jnp.int32, (CHUNK, CHUNK), 1)
    tri = row_id >= col_id
    lf_rows = jnp.concatenate([log_f[:, c * CHUNK:(c + 1) * CHUNK] for c in range(n_chunks)], axis=0)
    bcum_rows = _dot_f32_lhs(lf_rows, jnp.where(row_id <= col_id, 1.0, 0.0))
    li_rows = jnp.concatenate([log_i[:, c * CHUNK:(c + 1) * CHUNK] for c in range(n_chunks)], axis=0)
    cc_rows = li_rows - bcum_rows
    cmax_rows = _lane_prefix_max(cc_rows)
    m_prev = m_st[...]
    for c in range(n_chunks):
        blk = slice(c * N_HEADS, (c + 1) * N_HEADS)
        bcum = bcum_rows[blk]
        cc = cc_rows[blk]
        mx = jnp.maximum(m_prev, cmax_rows[blk])
        mx_last = _last_lane(mx)
        rows = (cc, mx, jnp.exp(-(bcum + mx)), jnp.exp(cc - mx_last), jnp.exp(m_prev - mx_last), m_prev)
        for dst, val in zip((c_s, mx_s, en_s, ws_s, wc_s, mp_s), rows):
            for hh in range(N_HEADS):
                dst[c, hh] = val[hh:hh + 1]
        m_prev = _last_lane(bcum) + mx_last
    m_st[...] = m_prev

    def project(h, slot, part):
        dst = bufs[slot][part]
        start = part * N_HEADS * dv + h * dv
        out = _dot(xb_s[...], w_in_ref[:, start:start + dv])
        if part == 0:
            dst[0:CONV_HIST, :] = hist_s[h]
            dst[CONV_HIST:, :] = out
            hist_s[h] = dst[TILE:TILE + CONV_HIST, :]
        else:
            dst[...] = out

    def mix(h, slot, fillers):
        xm_s, z_s, o_s, xc_s, q_s, kt_s, v_s = bufs[slot]
        cw = cw_ref[h]
        xe = xm_s[...]
        xe1 = pltpu.roll(xe, 1, 0)
        pair = cw[1:2] * xe + cw[0:1] * xe1
        conv = cb_ref[h] + cw[3:4] * xe + cw[2:3] * xe1 + pltpu.roll(pair, 2, 0)
        xc = _silu(conv[CONV_HIST:])
        xc_s[...] = xc
        xc_b = xc.astype(MXU_DTYPE)
        q_s[...] = _dot(xc_b, wq_ref[h]) * q_scale
        kt_s[...] = _dot_nt(wkt_ref[h], xc_b)
        v_s[...] = _dot(xm_s[CONV_HIST:, :], wv_ref[h]).astype(v_s.dtype)

        for c in range(n_chunks):
            rows = slice(c * CHUNK, (c + 1) * CHUNK)
            m_in = mp_s[c, h][:, 0:1]
            mx_col = _column_of(mx_s[c, h])
            decay = jnp.where(tri, jnp.exp(c_s[c, h] - mx_col), 0.0)
            qc = q_s[rows, :]
            ktc = kt_s[:, rows]
            vc = v_s[rows, :]
            scores = _dot(qc, ktc)
            kts = ktc * ws_s[c, h]
            kv = _dot(kts, vc)
            if fillers[c] is not None:
                fillers[c]()
            s = scores * decay
            qs = qc * jnp.exp(m_in - mx_col)
            c_prev = cst_s[h]
            n_prev = nst_s[h]
            den = jnp.sum(s + qs * n_prev, axis=-1, keepdims=True)
            num = _dot(jnp.concatenate([s, qs], axis=1),
                       jnp.concatenate([vc, c_prev.astype(MXU_DTYPE)], axis=0))
            hv = num / jnp.maximum(jnp.abs(den), _column_of(en_s[c, h])[:, 0:1])
            wc = wc_s[c, h][:, 0:1]
            cst_s[h] = wc * c_prev + kv
            nst_s[h] = wc * n_prev + _row_of(jnp.sum(kts, axis=-1, keepdims=True))
            mu = jnp.mean(hv, axis=-1, keepdims=True)
            cen = hv - mu
            var = jnp.mean(cen * cen, axis=-1, keepdims=True)
            hn = cen * lax.rsqrt(var + LN_EPS) * ng_ref[h]
            hb = _sigmoid(o_s[rows, :]) * hn + sk_ref[h] * xc_s[rows, :]
            g_s[h, rows, :] = (hb * _silu(z_s[rows, :])).astype(g_s.dtype)

    half = acc_s.shape[1] // 2

    def out_project(h, part):
        cols = slice(part * half, (part + 1) * half)
        contrib = _dot(g_s[h], wo_ref[h, :, cols])
        if h == 0:
            acc_s[:, cols] = contrib
        else:
            acc_s[:, cols] += contrib

    for part in range(3):
        project(0, 0, part)
    for h in range(N_HEADS):
        slot = h % 2
        fillers = [None] * n_chunks
        if h + 1 < N_HEADS:
            project(h + 1, 1 - slot, 0)
            fillers[0] = functools.partial(project, h + 1, 1 - slot, 1)
            fillers[1] = functools.partial(project, h + 1, 1 - slot, 2)
        if h > 0:
            fillers[2] = functools.partial(out_project, h - 1, 0)
            fillers[3] = functools.partial(out_project, h - 1, 1)
        mix(h, slot, fillers)
    branch = acc_s[...] + _dot(g_s[N_HEADS - 1], wo_ref[N_HEADS - 1])
    o_ref[...] = _post_norm(x_ref[...], branch, lg_ref[...], lb_ref[...], alpha)


def _resident(shape):
    zeros = (0,) * len(shape)
    return pl.BlockSpec(shape, lambda b, i: zeros, pipeline_mode=pl.Buffered(1))


def _mlstm_layer(x, w_in, conv_w, conv_b, w_q, w_k, w_v, b_i, b_f, norm_g, skip, w_out,
                 ln_g, ln_b, alpha):
    B, S, D = x.shape
    H = N_HEADS
    dv = w_v.shape[-1]
    dk = w_q.shape[-1]
    E = H * dv
    assert S % TILE == 0 and TILE % CHUNK == 0 and CHUNK == dk and H % 2 == 0
    assert conv_w.shape[0] == CONV_K == 4 and CONV_HIST >= CONV_K - 1
    wdt = MXU_DTYPE
    w_main = w_in.astype(wdt)
    w_gate = w_in[:, 3 * E:].T.astype(wdt)
    b_gate = jnp.broadcast_to(jnp.concatenate([b_i, b_f])[:, None], (2 * H, 128)).astype(jnp.float32)
    args = (
        x, w_main, w_gate, b_gate,
        conv_w.reshape(CONV_K, H, dv).transpose(1, 0, 2), conv_b.reshape(H, 1, dv),
        w_q.astype(wdt), w_k.transpose(0, 2, 1).astype(wdt), w_v.astype(wdt),
        norm_g.reshape(H, 1, dv), skip.reshape(H, 1, dv),
        w_out.reshape(H, dv, D).astype(wdt), ln_g.reshape(1, D), ln_b.reshape(1, D),
    )
    x_spec = pl.BlockSpec((None, TILE, D), lambda b, i: (b, i, 0))
    in_specs = [x_spec, _resident((D, 3 * E))] + [_resident(a.shape) for a in args[2:]]
    n_chunks = TILE // CHUNK
    row_scratch = pltpu.VMEM((n_chunks, H, 1, CHUNK), jnp.float32)
    head_bufs = [
        pltpu.VMEM((TILE + CONV_HIST, dv), jnp.float32),
        pltpu.VMEM((TILE, dv), jnp.float32),
        pltpu.VMEM((TILE, dv), jnp.float32),
        pltpu.VMEM((TILE, dv), jnp.float32),
        pltpu.VMEM((TILE, dk), jnp.float32),
        pltpu.VMEM((dk, TILE), jnp.float32),
        pltpu.VMEM((TILE, dv), wdt),
    ]
    scratch = [
        pltpu.VMEM((TILE, D), wdt),
        pltpu.VMEM((H, TILE, dv), wdt),
        pltpu.VMEM((TILE, D), jnp.float32),
        row_scratch, row_scratch, row_scratch, row_scratch, row_scratch, row_scratch,
        pltpu.VMEM((H, CHUNK), jnp.float32),
        pltpu.VMEM((H, dk, dv), jnp.float32),
        pltpu.VMEM((H, 1, dk), jnp.float32),
        pltpu.VMEM((H, CONV_HIST, dv), jnp.float32),
    ] + head_bufs + head_bufs
    kern = functools.partial(_mlstm_layer_kernel, alpha=alpha, q_scale=float(dk) ** -0.5)
    return pl.pallas_call(
        kern,
        grid=(B, S // TILE),
        in_specs=in_specs,
        out_specs=pl.BlockSpec((None, TILE, D), lambda b, i: (b, i, 0)),
        out_shape=jax.ShapeDtypeStruct((B, S, D), x.dtype),
        scratch_shapes=scratch,
        compiler_params=pltpu.CompilerParams(
            dimension_semantics=("arbitrary", "arbitrary"), vmem_limit_bytes=VMEM_LIMIT_BYTES),
        name="mlstm_layer",
    )(*args)


def _pool_layer_kernel(x_ref, w_in_ref, wg_ref, sc_ref, wo_ref, lg_ref, lb_ref, o_ref,
                       xb_s, u_s, gt_s, hist_s, *, alpha):
    n_groups, _, cg = u_s.shape
    first_tile = pl.program_id(1) == 0

    @pl.when(first_tile)
    def _():
        hist_s[...] = jnp.zeros_like(hist_s)

    xb_s[...] = x_ref[...].astype(xb_s.dtype)
    head_row = lax.broadcasted_iota(jnp.int32, (POOL_HIST, cg), 0)

    for g, win in enumerate(POOL_WINDOWS):
        u = _dot(xb_s[...], w_in_ref[:, g * cg:(g + 1) * cg])
        z = _dot(xb_s[...], w_in_ref[:, n_groups * cg + g * cg:n_groups * cg + (g + 1) * cg])
        u_s[g, 0:POOL_HIST, :] = hist_s[g]
        u_s[g, POOL_HIST:, :] = u
        hist_s[g] = u_s[g, POOL_TILE:POOL_TILE + POOL_HIST, :]
        wsum = u_s[g]
        span = 1
        while span < win:
            wsum = wsum + pltpu.roll(wsum, span, 0)
            span *= 2
        wsum = wsum[POOL_HIST:]
        cnt = jnp.where(first_tile, jnp.minimum(head_row + 1, win), win).astype(jnp.float32)
        pooled = jnp.concatenate(
            [wsum[:POOL_HIST] / cnt - u[:POOL_HIST], wsum[POOL_HIST:] * (1.0 / win) - u[POOL_HIST:]],
            axis=0)
        mixed = _dot(pooled, wg_ref[g]) * sc_ref[g]
        gt_s[:, g * cg:(g + 1) * cg] = (mixed * _silu(z)).astype(gt_s.dtype)

    n_rows = POOL_TILE // POOL_OUT_BLOCKS
    for r in range(POOL_OUT_BLOCKS):
        rows = slice(r * n_rows, (r + 1) * n_rows)
        branch = _dot(gt_s[rows, :], wo_ref[...])
        o_ref[rows, :] = _post_norm(x_ref[rows, :], branch, lg_ref[...], lb_ref[...], alpha)


def _pool_layer(x, w_in, w_group, scale, w_out, ln_g, ln_b, alpha):
    B, S, D = x.shape
    G, cg, _ = w_group.shape
    E = G * cg
    assert S % POOL_TILE == 0 and G == len(POOL_WINDOWS) and max(POOL_WINDOWS) <= POOL_HIST
    wdt = MXU_DTYPE
    args = (x, w_in.astype(wdt), w_group.astype(wdt), scale.reshape(G, 1, cg),
            w_out.astype(wdt), ln_g.reshape(1, D), ln_b.reshape(1, D))
    in_specs = ([pl.BlockSpec((None, POOL_TILE, D), lambda b, i: (b, i, 0))]
                + [_resident(a.shape) for a in args[1:]])
    scratch = [
        pltpu.VMEM((POOL_TILE, D), wdt),
        pltpu.VMEM((G, POOL_TILE + POOL_HIST, cg), jnp.float32),
        pltpu.VMEM((POOL_TILE, E), wdt),
        pltpu.VMEM((G, POOL_HIST, cg), jnp.float32),
    ]
    return pl.pallas_call(
        functools.partial(_pool_layer_kernel, alpha=alpha),
        grid=(B, S // POOL_TILE),
        in_specs=in_specs,
        out_specs=pl.BlockSpec((None, POOL_TILE, D), lambda b, i: (b, i, 0)),
        out_shape=jax.ShapeDtypeStruct((B, S, D), x.dtype),
        scratch_shapes=scratch,
        compiler_params=pltpu.CompilerParams(
            dimension_semantics=("arbitrary", "arbitrary"), vmem_limit_bytes=VMEM_LIMIT_BYTES),
        name="pool_layer",
    )(*args)


def kernel(x, mlstm_w_in, mlstm_conv_w, mlstm_conv_b, mlstm_w_q, mlstm_w_k, mlstm_w_v, mlstm_b_i,
           mlstm_b_f, mlstm_norm_g, mlstm_skip, mlstm_w_out, pool_w_in, pool_w_group, pool_scale,
           pool_w_out, post_ln_g, post_ln_b):
    depth = post_ln_g.shape[0]
    alpha = (2.0 * depth) ** 0.25
    for i in range(depth):
        j = i // 2
        if i % 2 == 0:
            x = _mlstm_layer(x, mlstm_w_in[j], mlstm_conv_w[j], mlstm_conv_b[j], mlstm_w_q[j],
                             mlstm_w_k[j], mlstm_w_v[j], mlstm_b_i[j], mlstm_b_f[j],
                             mlstm_norm_g[j], mlstm_skip[j], mlstm_w_out[j],
                             post_ln_g[i], post_ln_b[i], alpha)
        else:
            x = _pool_layer(x, pool_w_in[j], pool_w_group[j], pool_scale[j], pool_w_out[j],
                            post_ln_g[i], post_ln_b[i], alpha)
    return x
```

```python
import functools

import jax
import jax.numpy as jnp
from jax import lax
from jax.experimental import pallas as pl
from jax.experimental.pallas import tpu as pltpu

N_HEADS = 8
CONV_K = 4
CHUNK = 128
POOL_WINDOWS = (2, 4, 8, 16)
LN_EPS = 1e-5
TILE = 512
POOL_TILE = 1024
POOL_OUT_BLOCKS = 4
CONV_HIST = 8
POOL_HIST = 16
MXU_DTYPE = jnp.bfloat16
VMEM_LIMIT_BYTES = 52 * 1024 * 1024


NEG_LOG2_E = -1.4426950408889634


def _sigmoid(x):
    return 1.0 / (1.0 + jnp.exp2(x * NEG_LOG2_E))


def _silu(x):
    return x * _sigmoid(x)


def _log_sigmoid(x):
    return jnp.minimum(x, 0.0) - jnp.log1p(jnp.exp(-jnp.abs(x)))


def _dot(a, b):
    return jnp.dot(a.astype(MXU_DTYPE), b.astype(MXU_DTYPE), preferred_element_type=jnp.float32)


def _dot_nt(a, b):
    return lax.dot_general(a.astype(MXU_DTYPE), b.astype(MXU_DTYPE),
                           (((1,), (1,)), ((), ())), preferred_element_type=jnp.float32)


def _dot_f32_lhs(a, b):
    r = a.shape[0]
    hi = a.astype(jnp.bfloat16).astype(jnp.float32)
    rest = a - hi
    mid = rest.astype(jnp.bfloat16).astype(jnp.float32)
    parts = jnp.concatenate([hi, mid, rest - mid], axis=0).astype(jnp.bfloat16)
    out = jnp.dot(parts, b.astype(jnp.bfloat16), preferred_element_type=jnp.float32)
    return out[:r] + out[r:2 * r] + out[2 * r:]


def _lane_prefix_max(x):
    lane = lax.broadcasted_iota(jnp.int32, x.shape, 1)
    shift = 1
    while shift < x.shape[1]:
        x = jnp.maximum(x, jnp.where(lane >= shift, pltpu.roll(x, shift, 1), -jnp.inf))
        shift *= 2
    return x


def _column_of(row):
    n = row.shape[1]
    return jnp.broadcast_to(row, (n, n)).T


def _last_lane(x):
    return jnp.broadcast_to(x[:, x.shape[1] - 1:], x.shape)


def _row_of(col):
    n = col.shape[0]
    return jnp.broadcast_to(col, (n, n)).T[0:1]


def _post_norm(x, branch, g, b, alpha):
    y = alpha * x + branch
    mu = jnp.mean(y, axis=-1, keepdims=True)
    cen = y - mu
    var = jnp.mean(cen * cen, axis=-1, keepdims=True)
    return cen * lax.rsqrt(var + LN_EPS) * g + b


def _mlstm_layer_kernel(x_ref, w_in_ref, wg_ref, bg_ref, cw_ref, cb_ref, wq_ref, wkt_ref, wv_ref,
                        ng_ref, sk_ref, wo_ref, lg_ref, lb_ref, o_ref,
                        xb_s, g_s, acc_s, c_s, mx_s, en_s, ws_s, wc_s, mp_s,
                        m_st, cst_s, nst_s, hist_s, *head_bufs, alpha, q_scale):
    n_chunks = TILE // CHUNK
    dv = g_s.shape[2]
    bufs = (head_bufs[:len(head_bufs) // 2], head_bufs[len(head_bufs) // 2:])
    first_tile = pl.program_id(1) == 0

    @pl.when(first_tile)
    def _():
        m_st[...] = jnp.zeros_like(m_st)
        cst_s[...] = jnp.zeros_like(cst_s)
        nst_s[...] = jnp.zeros_like(nst_s)
        hist_s[...] = jnp.zeros_like(hist_s)

    xb_s[...] = x_ref[...].astype(xb_s.dtype)

    gates = _dot_nt(wg_ref[...], xb_s[...]) + bg_ref[:, 0:1]
    log_i = gates[:N_HEADS]
    log_f = _log_sigmoid(gates[N_HEADS:])
    row_id = lax.broadcasted_iota(jnp.int32, (CHUNK, CHUNK), 0)
    col_id = lax.broadcasted_iota(jnp.int32, (CHUNK, CHUNK), 1)
    tri = row_id >= col_id
    lf_rows = jnp.concatenate([log_f[:, c * CHUNK:(c + 1) * CHUNK] for c in range(n_chunks)], axis=0)
    bcum_rows = _dot_f32_lhs(lf_rows, jnp.where(row_id <= col_id, 1.0, 0.0))
    li_rows = jnp.concatenate([log_i[:, c * CHUNK:(c + 1) * CHUNK] for c in range(n_chunks)], axis=0)
    cc_rows = li_rows - bcum_rows
    cmax_rows = _lane_prefix_max(cc_rows)
    m_prev = m_st[...]
    for c in range(n_chunks):
        blk = slice(c * N_HEADS, (c + 1) * N_HEADS)
        bcum = bcum_rows[blk]
        cc = cc_rows[blk]
        mx = jnp.maximum(m_prev, cmax_rows[blk])
        mx_last = _last_lane(mx)
        rows = (cc, mx, jnp.exp(-(bcum + mx)), jnp.exp(cc - mx_last), jnp.exp(m_prev - mx_last), m_prev)
        for dst, val in zip((c_s, mx_s, en_s, ws_s, wc_s, mp_s), rows):
            for hh in range(N_HEADS):
                dst[c, hh] = val[hh:hh + 1]
        m_prev = _last_lane(bcum) + mx_last
    m_st[...] = m_prev

    def project(h, slot, part):
        dst = bufs[slot][part]
        start = part * N_HEADS * dv + h * dv
        out = _dot(xb_s[...], w_in_ref[:, start:start + dv])
        if part == 0:
            dst[0:CONV_HIST, :] = hist_s[h]
            dst[CONV_HIST:, :] = out
            hist_s[h] = dst[TILE:TILE + CONV_HIST, :]
        else:
            dst[...] = out

    def mix(h, slot, fillers):
        xm_s, z_s, o_s, xc_s, q_s, kt_s, v_s = bufs[slot]
        cw = cw_ref[h]
        xe = xm_s[...]
        xe1 = pltpu.roll(xe, 1, 0)
        pair = cw[1:2] * xe + cw[0:1] * xe1
        conv = cb_ref[h] + cw[3:4] * xe + cw[2:3] * xe1 + pltpu.roll(pair, 2, 0)
        xc = _silu(conv[CONV_HIST:])
        xc_s[...] = xc
        xc_b = xc.astype(MXU_DTYPE)
        q_s[...] = _dot(xc_b, wq_ref[h]) * q_scale
        kt_s[...] = _dot_nt(wkt_ref[h], xc_b)
        v_s[...] = _dot(xm_s[CONV_HIST:, :], wv_ref[h]).astype(v_s.dtype)

        for c in range(n_chunks):
            rows = slice(c * CHUNK, (c + 1) * CHUNK)
            m_in = mp_s[c, h][:, 0:1]
            mx_col = _column_of(mx_s[c, h])
            decay = jnp.where(tri, jnp.exp(c_s[c, h] - mx_col), 0.0)
            qc = q_s[rows, :]
            ktc = kt_s[:, rows]
            vc = v_s[rows, :]
            if c % 2 == 0:
                two = slice(c * CHUNK, (c + 2) * CHUNK)
                pair_scores = _dot(q_s[two, :], kt_s[:, two])
            block = slice((c % 2) * CHUNK, (c % 2 + 1) * CHUNK)
            scores = pair_scores[block, block]
            kts = ktc * ws_s[c, h]
            kv = _dot(kts, vc)
            if fillers[c] is not None:
                fillers[c]()
            s = scores * decay
            qs = qc * jnp.exp(m_in - mx_col)
            c_prev = cst_s[h]
            n_prev = nst_s[h]
            den = jnp.sum(s + qs * n_prev, axis=-1, keepdims=True)
            num = _dot(jnp.concatenate([s, qs], axis=1),
                       jnp.concatenate([vc, c_prev.astype(MXU_DTYPE)], axis=0))
            hv = num / jnp.maximum(jnp.abs(den), _column_of(en_s[c, h])[:, 0:1])
            wc = wc_s[c, h][:, 0:1]
            cst_s[h] = wc * c_prev + kv
            nst_s[h] = wc * n_prev + _row_of(jnp.sum(kts, axis=-1, keepdims=True))
            mu = jnp.mean(hv, axis=-1, keepdims=True)
            cen = hv - mu
            var = jnp.mean(cen * cen, axis=-1, keepdims=True)
            hn = cen * lax.rsqrt(var + LN_EPS) * ng_ref[h]
            hb = _sigmoid(o_s[rows, :]) * hn + sk_ref[h] * xc_s[rows, :]
            g_s[h, rows, :] = (hb * _silu(z_s[rows, :])).astype(g_s.dtype)

    half = acc_s.shape[1] // 2

    def out_project(h, part):
        cols = slice(part * half, (part + 1) * half)
        contrib = _dot(g_s[h], wo_ref[h, :, cols])
        if h == 0:
            acc_s[:, cols] = contrib
        else:
            acc_s[:, cols] += contrib

    for part in range(3):
        project(0, 0, part)
    for h in range(N_HEADS):
        slot = h % 2
        fillers = [None] * n_chunks
        if h + 1 < N_HEADS:
            project(h + 1, 1 - slot, 0)
            fillers[0] = functools.partial(project, h + 1, 1 - slot, 1)
            fillers[1] = functools.partial(project, h + 1, 1 - slot, 2)
        if h > 0:
            fillers[2] = functools.partial(out_project, h - 1, 0)
            fillers[3] = functools.partial(out_project, h - 1, 1)
        mix(h, slot, fillers)
    branch = acc_s[...] + _dot(g_s[N_HEADS - 1], wo_ref[N_HEADS - 1])
    o_ref[...] = _post_norm(x_ref[...], branch, lg_ref[...], lb_ref[...], alpha)


def _resident(shape):
    zeros = (0,) * len(shape)
    return pl.BlockSpec(shape, lambda b, i: zeros, pipeline_mode=pl.Buffered(1))


def _mlstm_layer(x, w_in, conv_w, conv_b, w_q, w_k, w_v, b_i, b_f, norm_g, skip, w_out,
                 ln_g, ln_b, alpha):
    B, S, D = x.shape
    H = N_HEADS
    dv = w_v.shape[-1]
    dk = w_q.shape[-1]
    E = H * dv
    assert S % TILE == 0 and TILE % CHUNK == 0 and CHUNK == dk and H % 2 == 0
    assert conv_w.shape[0] == CONV_K == 4 and CONV_HIST >= CONV_K - 1
    wdt = MXU_DTYPE
    w_main = w_in.astype(wdt)
    w_gate = w_in[:, 3 * E:].T.astype(wdt)
    b_gate = jnp.broadcast_to(jnp.concatenate([b_i, b_f])[:, None], (2 * H, 128)).astype(jnp.float32)
    args = (
        x, w_main, w_gate, b_gate,
        conv_w.reshape(CONV_K, H, dv).transpose(1, 0, 2), conv_b.reshape(H, 1, dv),
        w_q.astype(wdt), w_k.transpose(0, 2, 1).astype(wdt), w_v.astype(wdt),
        norm_g.reshape(H, 1, dv), skip.reshape(H, 1, dv),
        w_out.reshape(H, dv, D).astype(wdt), ln_g.reshape(1, D), ln_b.reshape(1, D),
    )
    x_spec = pl.BlockSpec((None, TILE, D), lambda b, i: (b, i, 0))
    in_specs = [x_spec, _resident((D, 3 * E))] + [_resident(a.shape) for a in args[2:]]
    n_chunks = TILE // CHUNK
    row_scratch = pltpu.VMEM((n_chunks, H, 1, CHUNK), jnp.float32)
    head_bufs = [
        pltpu.VMEM((TILE + CONV_HIST, dv), jnp.float32),
        pltpu.VMEM((TILE, dv), jnp.float32),
        pltpu.VMEM((TILE, dv), jnp.float32),
        pltpu.VMEM((TILE, dv), jnp.float32),
        pltpu.VMEM((TILE, dk), jnp.float32),
        pltpu.VMEM((dk, TILE), jnp.float32),
        pltpu.VMEM((TILE, dv), wdt),
    ]
    scratch = [
        pltpu.VMEM((TILE, D), wdt),
        pltpu.VMEM((H, TILE, dv), wdt),
        pltpu.VMEM((TILE, D), jnp.float32),
        row_scratch, row_scratch, row_scratch, row_scratch, row_scratch, row_scratch,
        pltpu.VMEM((H, CHUNK), jnp.float32),
        pltpu.VMEM((H, dk, dv), jnp.float32),
        pltpu.VMEM((H, 1, dk), jnp.float32),
        pltpu.VMEM((H, CONV_HIST, dv), jnp.float32),
    ] + head_bufs + head_bufs
    kern = functools.partial(_mlstm_layer_kernel, alpha=alpha, q_scale=float(dk) ** -0.5)
    return pl.pallas_call(
        kern,
        grid=(B, S // TILE),
        in_specs=in_specs,
        out_specs=pl.BlockSpec((None, TILE, D), lambda b, i: (b, i, 0)),
        out_shape=jax.ShapeDtypeStruct((B, S, D), x.dtype),
        scratch_shapes=scratch,
        compiler_params=pltpu.CompilerParams(
            dimension_semantics=("arbitrary", "arbitrary"), vmem_limit_bytes=VMEM_LIMIT_BYTES),
        name="mlstm_layer",
    )(*args)


def _pool_layer_kernel(x_ref, w_in_ref, wg_ref, sc_ref, wo_ref, lg_ref, lb_ref, o_ref,
                       xb_s, u_s, gt_s, hist_s, *, alpha):
    n_groups, _, cg = u_s.shape
    first_tile = pl.program_id(1) == 0

    @pl.when(first_tile)
    def _():
        hist_s[...] = jnp.zeros_like(hist_s)

    xb_s[...] = x_ref[...].astype(xb_s.dtype)
    head_row = lax.broadcasted_iota(jnp.int32, (POOL_HIST, cg), 0)

    for g, win in enumerate(POOL_WINDOWS):
        u = _dot(xb_s[...], w_in_ref[:, g * cg:(g + 1) * cg])
        z = _dot(xb_s[...], w_in_ref[:, n_groups * cg + g * cg:n_groups * cg + (g + 1) * cg])
        u_s[g, 0:POOL_HIST, :] = hist_s[g]
        u_s[g, POOL_HIST:, :] = u
        hist_s[g] = u_s[g, POOL_TILE:POOL_TILE + POOL_HIST, :]
        wsum = u_s[g]
        span = 1
        while span < win:
            wsum = wsum + pltpu.roll(wsum, span, 0)
            span *= 2
        wsum = wsum[POOL_HIST:]
        cnt = jnp.where(first_tile, jnp.minimum(head_row + 1, win), win).astype(jnp.float32)
        pooled = jnp.concatenate(
            [wsum[:POOL_HIST] / cnt - u[:POOL_HIST], wsum[POOL_HIST:] * (1.0 / win) - u[POOL_HIST:]],
            axis=0)
        mixed = _dot(pooled, wg_ref[g]) * sc_ref[g]
        gt_s[:, g * cg:(g + 1) * cg] = (mixed * _silu(z)).astype(gt_s.dtype)

    n_rows = POOL_TILE // POOL_OUT_BLOCKS
    for r in range(POOL_OUT_BLOCKS):
        rows = slice(r * n_rows, (r + 1) * n_rows)
        branch = _dot(gt_s[rows, :], wo_ref[...])
        o_ref[rows, :] = _post_norm(x_ref[rows, :], branch, lg_ref[...], lb_ref[...], alpha)


def _pool_layer(x, w_in, w_group, scale, w_out, ln_g, ln_b, alpha):
    B, S, D = x.shape
    G, cg, _ = w_group.shape
    E = G * cg
    assert S % POOL_TILE == 0 and G == len(POOL_WINDOWS) and max(POOL_WINDOWS) <= POOL_HIST
    wdt = MXU_DTYPE
    args = (x, w_in.astype(wdt), w_group.astype(wdt), scale.reshape(G, 1, cg),
            w_out.astype(wdt), ln_g.reshape(1, D), ln_b.reshape(1, D))
    in_specs = ([pl.BlockSpec((None, POOL_TILE, D), lambda b, i: (b, i, 0))]
                + [_resident(a.shape) for a in args[1:]])
    scratch = [
        pltpu.VMEM((POOL_TILE, D), wdt),
        pltpu.VMEM((G, POOL_TILE + POOL_HIST, cg), jnp.float32),
        pltpu.VMEM((POOL_TILE, E), wdt),
        pltpu.VMEM((G, POOL_HIST, cg), jnp.float32),
    ]
    return pl.pallas_call(
        functools.partial(_pool_layer_kernel, alpha=alpha),
        grid=(B, S // POOL_TILE),
        in_specs=in_specs,
        out_specs=pl.BlockSpec((None, POOL_TILE, D), lambda b, i: (b, i, 0)),
        out_shape=jax.ShapeDtypeStruct((B, S, D), x.dtype),
        scratch_shapes=scratch,
        compiler_params=pltpu.CompilerParams(
            dimension_semantics=("arbitrary", "arbitrary"), vmem_limit_bytes=VMEM_LIMIT_BYTES),
        name="pool_layer",
    )(*args)


def kernel(x, mlstm_w_in, mlstm_conv_w, mlstm_conv_b, mlstm_w_q, mlstm_w_k, mlstm_w_v, mlstm_b_i,
           mlstm_b_f, mlstm_norm_g, mlstm_skip, mlstm_w_out, pool_w_in, pool_w_group, pool_scale,
           pool_w_out, post_ln_g, post_ln_b):
    depth = post_ln_g.shape[0]
    alpha = (2.0 * depth) ** 0.25
    for i in range(depth):
        j = i // 2
        if i % 2 == 0:
            x = _mlstm_layer(x, mlstm_w_in[j], mlstm_conv_w[j], mlstm_conv_b[j], mlstm_w_q[j],
                             mlstm_w_k[j], mlstm_w_v[j], mlstm_b_i[j], mlstm_b_f[j],
                             mlstm_norm_g[j], mlstm_skip[j], mlstm_w_out[j],
                             post_ln_g[i], post_ln_b[i], alpha)
        else:
            x = _pool_layer(x, pool_w_in[j], pool_w_group[j], pool_scale[j], pool_w_out[j],
                            post_ln_g[i], post_ln_b[i], alpha)
    return x
```
